```python
import math
import jax
import jax.numpy as jnp
from jax import lax
import numpy as np

D_MODEL = 1024
BATCH = 16
SEQ = 4096
DEPTH = 2
DEC_BATCH = 2
DEC_SEQ = 16384
PAST_LEN = 128

NORM_EPS = 1e-6
DN_HEADS = 4
DN_HEAD_DIM = 128
DN_WIDTH = DN_HEADS * DN_HEAD_DIM
DN_CONV = 5
DN_CHUNK = 64
S5_WIDTH = 512
S5_GROUP = 16
S5_GROUPS = S5_WIDTH // S5_GROUP
S5_STATE = 64
RW_HEAD_DIM = 64
RW_WIDTH = 512
RW_HEADS = RW_WIDTH // RW_HEAD_DIM
RW_DECAY_RANK = 64
RW_AAA_RANK = 64
RW_GATE_RANK = 128
RW_GN_EPS = 64e-5
N_BRANCH = 3
N_EXPERTS = 16
EXPERT_FF = 2048
CAPACITY_FACTOR = 2

DN_SIZES = (DN_WIDTH, DN_WIDTH, DN_WIDTH, DN_WIDTH, 2 * DN_HEADS, 2 * DN_HEADS)
RW_SIZES = (RW_WIDTH, RW_WIDTH, RW_WIDTH, 2 * RW_DECAY_RANK, 2 * RW_AAA_RANK, RW_GATE_RANK)
RW_COLS = sum(RW_SIZES)
IN_SIZES = DN_SIZES + (S5_WIDTH, RW_COLS, N_BRANCH * D_MODEL)
IN_COLS = sum(IN_SIZES)

kernel_name = 'hybrid_bidir_deltanet_s5_rwkv7_ecmoe'


def _split_points(sizes):
    pts, acc = [], 0
    for s in sizes[:-1]:
        acc += s
        pts.append(acc)
    return pts


def _rms_norm(x, g):
    xf = x.astype(jnp.float32)
    y = xf * lax.rsqrt(jnp.mean(xf * xf, axis=-1, keepdims=True) + NORM_EPS)
    return (y * g.astype(jnp.float32)).astype(x.dtype)


def _l2_normalize(x):
    return x * lax.rsqrt(jnp.sum(x * x, axis=-1, keepdims=True) + NORM_EPS)


def _centred_depthwise_conv(x, w):
    pad = (w.shape[0] - 1) // 2
    return lax.conv_general_dilated(
        x, w[:, None, :].astype(x.dtype), window_strides=(1,), padding=[(pad, pad)],
        dimension_numbers=('NWC', 'WIO', 'NWC'), feature_group_count=x.shape[-1])


def _centred_token_shift(x):
    prev = jnp.pad(x[:, :-1], ((0, 0), (1, 0), (0, 0)))
    nxt = jnp.pad(x[:, 1:], ((0, 0), (0, 1), (0, 0)))
    return 0.5 * (prev + nxt)


def _gated_delta_chunked(q, k, v, g, beta):
    bsz, seqlen, nh, dk = q.shape
    dv = v.shape[-1]
    nc = seqlen // DN_CHUNK

    def blocks(t):
        t = t.reshape((bsz, nc, DN_CHUNK, nh) + t.shape[3:])
        return jnp.moveaxis(t, (1, 3), (0, 2))

    q = blocks(q) * (dk ** -0.5)
    k, v, g, beta = blocks(k), blocks(v), blocks(g), blocks(beta)
    gc = jnp.cumsum(g, axis=-1)
    pos = jnp.arange(DN_CHUNK)
    incl = pos[:, None] >= pos[None, :]
    strict = pos[:, None] > pos[None, :]
    decay = jnp.exp(jnp.where(incl, gc[..., :, None] - gc[..., None, :], -jnp.inf))
    k_beta = k * beta[..., None]
    m = jnp.where(strict, jnp.einsum('nbhid,nbhjd->nbhij', k_beta, k) * decay, 0.0)
    eye = jnp.eye(DN_CHUNK, dtype=m.dtype)
    t_inv = lax.linalg.triangular_solve(m + eye, jnp.broadcast_to(eye, m.shape),
                                        left_side=True, lower=True, unit_diagonal=True)
    u = t_inv @ (v * beta[..., None])
    w = t_inv @ (k_beta * jnp.exp(gc)[..., None])
    a_qk = jnp.einsum('nbhid,nbhjd->nbhij', q, k) * decay
    q_dec = q * jnp.exp(gc)[..., None]
    k_dec = k * jnp.exp(gc[..., -1:] - gc)[..., None]
    chunk_decay = jnp.exp(gc[..., -1])[..., None, None]

    def step(s, xs):
        u_i, w_i, q_i, k_i, a_i, d_i = xs
        v_new = u_i - w_i @ s
        o_i = q_i @ s + a_i @ v_new
        s = s * d_i + jnp.swapaxes(k_i, -1, -2) @ v_new
        return s, o_i

    s0 = jnp.zeros((bsz, nh, dk, dv), q.dtype)
    _, o = lax.scan(step, s0, (u, w, q_dec, k_dec, a_qk, chunk_decay))
    return jnp.moveaxis(o, (0, 2), (1, 3)).reshape(bsz, seqlen, nh, dv)


def _deltanet_branch(q, k, v, z, beta_raw, alpha_raw, conv_w, a_log, dt_bias, norm_g):
    f32 = jnp.float32
    bsz, seqlen, _ = q.shape
    heads = lambda t: t.reshape(bsz, seqlen, DN_HEADS, DN_HEAD_DIM)
    qkv = jax.nn.silu(_centred_depthwise_conv(jnp.concatenate([q, k, v], axis=-1), conv_w))
    q, k, v = jnp.split(qkv.astype(f32), 3, axis=-1)
    q, k, v = _l2_normalize(heads(q)), _l2_normalize(heads(k)), heads(v)
    beta = jax.nn.sigmoid(beta_raw.astype(f32)).reshape(bsz, seqlen, 2, DN_HEADS)
    g = -jnp.exp(a_log.astype(f32)) * jax.nn.softplus(
        alpha_raw.astype(f32).reshape(bsz, seqlen, 2, DN_HEADS) + dt_bias.astype(f32))
    o_fwd = _gated_delta_chunked(q, k, v, g[:, :, 0], beta[:, :, 0])
    rev = lambda t: jnp.flip(t, axis=1)
    o_bwd = rev(_gated_delta_chunked(rev(q), rev(k), rev(v), rev(g[:, :, 1]), rev(beta[:, :, 1])))
    o = o_fwd + o_bwd
    o = (o * lax.rsqrt(jnp.mean(o * o, axis=-1, keepdims=True) + NORM_EPS) * norm_g
         * jax.nn.silu(heads(z.astype(f32))))
    return o.reshape(bsz, seqlen, DN_WIDTH)


def _complex_affine_combine(e1, e2):
    a1r, a1i, b1r, b1i = e1
    a2r, a2i, b2r, b2i = e2
    return (a2r * a1r - a2i * a1i, a2r * a1i + a2i * a1r,
            a2r * b1r - a2i * b1i + b2r, a2r * b1i + a2i * b1r + b2i)


def _s5_direction(u, lam_re, lam_im, log_step, b_re, b_im, c_re, c_im, reverse):
    step = jnp.exp(log_step)[:, None]
    mag = jnp.exp(lam_re * step)
    lbar_re, lbar_im = mag * jnp.cos(lam_im * step), mag * jnp.sin(lam_im * step)
    den = lam_re * lam_re + lam_im * lam_im
    num_re = lbar_re - 1.0
    f_re = (num_re * lam_re + lbar_im * lam_im) / den
    f_im = (lbar_im * lam_re - num_re * lam_im) / den
    bbar_re = f_re[..., None] * b_re - f_im[..., None] * b_im
    bbar_im = f_re[..., None] * b_im + f_im[..., None] * b_re
    bu_re = jnp.einsum('blgc,gpc->blgp', u, bbar_re)
    bu_im = jnp.einsum('blgc,gpc->blgp', u, bbar_im)
    a_re = jnp.broadcast_to(lbar_re, bu_re.shape)
    a_im = jnp.broadcast_to(lbar_im, bu_im.shape)
    _, _, h_re, h_im = lax.associative_scan(_complex_affine_combine, (a_re, a_im, bu_re, bu_im),
                                            reverse=reverse, axis=1)
    return jnp.einsum('blgp,gcp->blgc', h_re, c_re) - jnp.einsum('blgp,gcp->blgc', h_im, c_im)


def _s5_branch(u, lam_re, lam_im, log_step, b_re, b_im, c_re, c_im, d_skip, glu_w, glu_b):
    f32 = jnp.float32
    bsz, seqlen, _ = u.shape
    uf = u.astype(f32)
    ug = uf.reshape(bsz, seqlen, S5_GROUPS, S5_GROUP)
    y = uf * d_skip
    for dirn in range(2):
        y = y + _s5_direction(ug, lam_re[dirn].astype(f32), lam_im[dirn].astype(f32),
                              log_step[dirn].astype(f32), b_re[dirn].astype(f32), b_im[dirn].astype(f32),
                              c_re[dirn].astype(f32), c_im[dirn].astype(f32),
                              reverse=(dirn == 1)).reshape(bsz, seqlen, S5_WIDTH)
    y = jax.nn.gelu(y)
    return y * jax.nn.sigmoid(y @ glu_w + glu_b)


def _rwkv7_scan(r, decay, k, v, kk, a, reverse):
    bsz, _, nh, hd = r.shape

    def step(s, xs):
        r_t, w_t, k_t, v_t, kk_t, a_t = xs
        s_kk = jnp.einsum('bhvk,bhk->bhv', s, kk_t)
        s = (s * w_t[:, :, None, :] - s_kk[..., None] * (kk_t * a_t)[:, :, None, :]
             + v_t[..., None] * k_t[:, :, None, :])
        return s, jnp.einsum('bhvk,bhk->bhv', s, r_t)

    xs = tuple(jnp.swapaxes(t, 0, 1) for t in (r, decay, k, v, kk, a))
    _, y = lax.scan(step, jnp.zeros((bsz, nh, hd, hd), r.dtype), xs, reverse=reverse)
    return jnp.swapaxes(y, 0, 1)


def _rwkv7_branch(feats, mu, w0, w2, a0, a2, g2, k_k, k_a, r_k, ln_g, ln_b):
    bsz, seqlen, _ = feats.shape
    f = feats.astype(jnp.float32)
    f = f + (_centred_token_shift(f) - f) * mu
    r, k, v, w_lo, a_lo, g_lo = jnp.split(f, _split_points(RW_SIZES), axis=-1)
    heads = lambda t: t.reshape(bsz, seqlen, RW_HEADS, RW_HEAD_DIM)
    w_lo = w_lo.reshape(bsz, seqlen, 2, RW_DECAY_RANK)
    a_lo = a_lo.reshape(bsz, seqlen, 2, RW_AAA_RANK)
    kk = _l2_normalize(heads(k * k_k))
    gate = jax.nn.sigmoid(g_lo) @ g2
    r_h, v_h = heads(r), heads(v)
    ys, bonuses = [], []
    for dirn in range(2):
        w_log = -jax.nn.softplus(-(w0[dirn] + jnp.tanh(w_lo[:, :, dirn]) @ w2[dirn])) - 0.5
        decay = jnp.exp(-jnp.exp(w_log))
        a_lr = jax.nn.sigmoid(a0[dirn] + a_lo[:, :, dirn] @ a2[dirn])
        k_d = heads(k * (1.0 + (a_lr - 1.0) * k_a))
        ys.append(_rwkv7_scan(r_h, heads(decay), k_d, v_h, kk, heads(a_lr), reverse=(dirn == 1)))
        bonuses.append(jnp.sum(r_h * k_d * r_k, axis=-1, keepdims=True) * v_h)
    y = ys[0] + ys[1]
    mean = jnp.mean(y, axis=-1, keepdims=True)
    var = jnp.mean(jnp.square(y - mean), axis=-1, keepdims=True)
    y = ((y - mean) * lax.rsqrt(var + RW_GN_EPS)).reshape(bsz, seqlen, RW_WIDTH) * ln_g + ln_b
    y = y + (bonuses[0] + bonuses[1]).reshape(bsz, seqlen, RW_WIDTH)
    return y * gate


def _mixer_layer(n, p, layer):
    bsz, seqlen, _ = n.shape
    proj = n @ p['w_in'][layer]
    dq, dk_, dv_, dz, dbeta, dalpha, s5_u, rw_feats, gate_raw = jnp.split(
        proj, _split_points(IN_SIZES), axis=-1)
    o_a = _deltanet_branch(dq, dk_, dv_, dz, dbeta, dalpha, p['dn_conv_w'][layer], p['dn_a_log'][layer],
                           p['dn_dt_bias'][layer], p['dn_norm_g'][layer])
    o_b = _s5_branch(s5_u, p['s5_lam_re'][layer], p['s5_lam_im'][layer], p['s5_log_step'][layer],
                     p['s5_b_re'][layer], p['s5_b_im'][layer], p['s5_c_re'][layer], p['s5_c_im'][layer],
                     p['s5_d'][layer], p['s5_glu_w'][layer], p['s5_glu_b'][layer])
    o_c = _rwkv7_branch(rw_feats, p['rw_mu'][layer], p['rw_w0'][layer], p['rw_w2'][layer], p['rw_a0'][layer],
                        p['rw_a2'][layer], p['rw_g2'][layer], p['rw_k_k'][layer], p['rw_k_a'][layer],
                        p['rw_r_k'][layer], p['rw_ln_g'][layer], p['rw_ln_b'][layer])
    gates = jax.nn.sigmoid(gate_raw.astype(jnp.float32)).reshape(bsz, seqlen, N_BRANCH, D_MODEL)
    merged = (gates[:, :, 0] * (o_a @ p['dn_proj'][layer])
              + gates[:, :, 1] * (o_b @ p['s5_proj'][layer])
              + gates[:, :, 2] * (o_c @ p['rw_proj'][layer]))
    return (merged @ p['w_out'][layer]).astype(n.dtype)


def _expert_choice_ffn(h, router_w, w1, w3, w2):
    bsz, seqlen, dm = h.shape
    ht = h.reshape(-1, dm)
    n_tok = ht.shape[0]
    cap = CAPACITY_FACTOR * n_tok // N_EXPERTS
    aff = jax.nn.softmax((ht @ router_w).astype(jnp.float32), axis=-1)
    gate, idx = lax.top_k(aff.T, cap)
    xin = ht[idx]
    hid = jax.nn.silu(jnp.einsum('ecd,edf->ecf', xin, w1)) * jnp.einsum('ecd,edf->ecf', xin, w3)
    out = jnp.einsum('ecf,efd->ecd', hid, w2) * gate[..., None]
    y = jnp.zeros((n_tok, dm), out.dtype).at[idx.reshape(-1)].add(out.reshape(-1, dm))
    return y.reshape(bsz, seqlen, dm)


def _trunk(x, p):
    h = x
    for layer in range(DEPTH):
        h = h + _mixer_layer(_rms_norm(h, p['norm1_g'][layer]), p, layer)
        h = h + _expert_choice_ffn(_rms_norm(h, p['norm2_g'][layer]), p['router_w'][layer],
                                   p['expert_w1'][layer], p['expert_w3'][layer], p['expert_w2'][layer])
    return _rms_norm(h, p['final_norm_g'])


def setup_inputs(seed: int = 0) -> dict:
    key = jax.random.key(seed)
    keys = iter(jax.random.split(key, 64))
    f32 = jnp.float32

    def normal(shape, scale):
        return scale * jax.random.normal(next(keys), shape, f32)

    def uniform(shape, lo, hi):
        return jax.random.uniform(next(keys), shape, f32, lo, hi)

    L = DEPTH
    dt = jnp.exp(uniform((L, 2, DN_HEADS), math.log(1e-3), math.log(1e-1)))
    return {
        'x_prompt': normal((BATCH, SEQ, D_MODEL), 1.0),
        'x_sample': normal((DEC_BATCH, DEC_SEQ, D_MODEL), 1.0),
        'norm1_g': 1.0 + normal((L, D_MODEL), 0.01),
        'norm2_g': 1.0 + normal((L, D_MODEL), 0.01),
        'final_norm_g': 1.0 + normal((D_MODEL,), 0.01),
        'w_in': normal((L, D_MODEL, IN_COLS), D_MODEL ** -0.5),
        'dn_conv_w': normal((L, DN_CONV, 3 * DN_WIDTH), DN_CONV ** -0.5),
        'dn_a_log': jnp.log(uniform((L, 2, DN_HEADS), 1.0, 16.0)),
        'dn_dt_bias': dt + jnp.log(-jnp.expm1(-dt)),
        'dn_norm_g': 1.0 + normal((L, DN_HEAD_DIM), 0.01),
        'dn_proj': normal((L, DN_WIDTH, D_MODEL), DN_WIDTH ** -0.5),
        's5_lam_re': -0.5 + normal((L, 2, S5_GROUPS, S5_STATE), 0.01),
        's5_lam_im': jnp.broadcast_to(math.pi * jnp.arange(S5_STATE, dtype=f32), (L, 2, S5_GROUPS, S5_STATE)),
        's5_log_step': uniform((L, 2, S5_GROUPS), math.log(1e-3), math.log(1e-1)),
        's5_b_re': normal((L, 2, S5_GROUPS, S5_STATE, S5_GROUP), (2 * S5_GROUP) ** -0.5),
        's5_b_im': normal((L, 2, S5_GROUPS, S5_STATE, S5_GROUP), (2 * S5_GROUP) ** -0.5),
        's5_c_re': normal((L, 2, S5_GROUPS, S5_GROUP, S5_STATE), (2 * S5_STATE) ** -0.5),
        's5_c_im': normal((L, 2, S5_GROUPS, S5_GROUP, S5_STATE), (2 * S5_STATE) ** -0.5),
        's5_d': normal((L, S5_WIDTH), 1.0),
        's5_glu_w': normal((L, S5_WIDTH, S5_WIDTH), S5_WIDTH ** -0.5),
        's5_glu_b': normal((L, S5_WIDTH), 0.01),
        's5_proj': normal((L, S5_WIDTH, D_MODEL), S5_WIDTH ** -0.5),
        'rw_mu': uniform((L, RW_COLS), 0.0, 1.0),
        'rw_w0': uniform((L, 2, RW_WIDTH), -6.5, -1.5),
        'rw_w2': normal((L, 2, RW_DECAY_RANK, RW_WIDTH), 0.5 * RW_DECAY_RANK ** -0.5),
        'rw_a0': normal((L, 2, RW_WIDTH), 0.1),
        'rw_a2': normal((L, 2, RW_AAA_RANK, RW_WIDTH), RW_AAA_RANK ** -0.5),
        'rw_g2': normal((L, RW_GATE_RANK, RW_WIDTH), RW_GATE_RANK ** -0.5),
        'rw_k_k': 0.85 + normal((L, RW_WIDTH), 0.01),
        'rw_k_a': 1.0 + normal((L, RW_WIDTH), 0.01),
        'rw_r_k': normal((L, RW_HEADS, RW_HEAD_DIM), 0.1),
        'rw_ln_g': 1.0 + normal((L, RW_WIDTH), 0.01),
        'rw_ln_b': normal((L, RW_WIDTH), 0.01),
        'rw_proj': normal((L, RW_WIDTH, D_MODEL), RW_WIDTH ** -0.5),
        'w_out': normal((L, D_MODEL, D_MODEL), D_MODEL ** -0.5),
        'router_w': normal((L, D_MODEL, N_EXPERTS), D_MODEL ** -0.5),
        'expert_w1': normal((L, N_EXPERTS, D_MODEL, EXPERT_FF), D_MODEL ** -0.5),
        'expert_w3': normal((L, N_EXPERTS, D_MODEL, EXPERT_FF), D_MODEL ** -0.5),
        'expert_w2': normal((L, N_EXPERTS, EXPERT_FF, D_MODEL), EXPERT_FF ** -0.5),
    }


def reference(x_prompt, x_sample, norm1_g, norm2_g, final_norm_g, w_in, dn_conv_w, dn_a_log, dn_dt_bias,
              dn_norm_g, dn_proj, s5_lam_re, s5_lam_im, s5_log_step, s5_b_re, s5_b_im, s5_c_re, s5_c_im,
              s5_d, s5_glu_w, s5_glu_b, s5_proj, rw_mu, rw_w0, rw_w2, rw_a0, rw_a2, rw_g2, rw_k_k, rw_k_a,
              rw_r_k, rw_ln_g, rw_ln_b, rw_proj, w_out, router_w, expert_w1, expert_w3, expert_w2):
    p = {
        'norm1_g': norm1_g, 'norm2_g': norm2_g, 'final_norm_g': final_norm_g, 'w_in': w_in,
        'dn_conv_w': dn_conv_w, 'dn_a_log': dn_a_log, 'dn_dt_bias': dn_dt_bias, 'dn_norm_g': dn_norm_g,
        'dn_proj': dn_proj, 's5_lam_re': s5_lam_re, 's5_lam_im': s5_lam_im, 's5_log_step': s5_log_step,
        's5_b_re': s5_b_re, 's5_b_im': s5_b_im, 's5_c_re': s5_c_re, 's5_c_im': s5_c_im, 's5_d': s5_d,
        's5_glu_w': s5_glu_w, 's5_glu_b': s5_glu_b, 's5_proj': s5_proj, 'rw_mu': rw_mu, 'rw_w0': rw_w0,
        'rw_w2': rw_w2, 'rw_a0': rw_a0, 'rw_a2': rw_a2, 'rw_g2': rw_g2, 'rw_k_k': rw_k_k, 'rw_k_a': rw_k_a,
        'rw_r_k': rw_r_k, 'rw_ln_g': rw_ln_g, 'rw_ln_b': rw_ln_b, 'rw_proj': rw_proj, 'w_out': w_out,
        'router_w': router_w, 'expert_w1': expert_w1, 'expert_w3': expert_w3, 'expert_w2': expert_w2,
    }
    y_prompt = _trunk(x_prompt, p)
    y_sample = _trunk(x_sample, p)
    return (y_prompt, y_sample)
```

```python
import functools
import math

import jax
import jax.numpy as jnp
from jax import lax
from jax.experimental import pallas as pl
from jax.experimental.pallas import tpu as pltpu

F32 = jnp.float32
BF16 = jnp.bfloat16
HI = lax.Precision.HIGHEST

D_MODEL = 1024
NORM_EPS = 1e-6
CHUNK = 64
DN_HEADS, DN_HEAD_DIM, DN_WIDTH, DN_CONV = 4, 128, 512, 5
S5_WIDTH, S5_GROUP, S5_GROUPS, S5_STATE = 512, 16, 32, 64
RW_WIDTH, RW_HEAD_DIM, RW_HEADS = 512, 64, 8
RW_DECAY_RANK, RW_AAA_RANK, RW_GATE_RANK = 64, 64, 128
RW_GN_EPS = 64e-5
RW_COLS = 3 * RW_WIDTH + 2 * RW_DECAY_RANK + 2 * RW_AAA_RANK + RW_GATE_RANK
N_EXPERTS, EXPERT_FF, CAPACITY_FACTOR = 16, 2048, 2
VMEM_LIMIT = 56 * 1024 * 1024

COL_GATE, COL_QKV, COL_Z, COL_S5, COL_BA, COL_RW, PROJ_COLS = 0, 3072, 4608, 5120, 5632, 6144, 8192
RW_PAD = 2048


def _proj_source_columns():
    import numpy as np
    w = DN_WIDTH
    off_q, off_z, off_beta = 0, 3 * w, 4 * w
    off_alpha = off_beta + 2 * DN_HEADS
    off_s5 = off_alpha + 2 * DN_HEADS
    off_rw = off_s5 + S5_WIDTH
    off_gate = off_rw + RW_COLS
    src = np.full((PROJ_COLS,), -1, np.int32)
    src[COL_GATE:COL_GATE + 3 * D_MODEL] = off_gate + np.arange(3 * D_MODEL)
    src[COL_QKV:COL_QKV + 3 * w] = off_q + np.arange(3 * w)
    src[COL_Z:COL_Z + w] = off_z + np.arange(w)
    src[COL_S5:COL_S5 + S5_WIDTH] = off_s5 + np.arange(S5_WIDTH)
    lane = np.arange(16)
    src[COL_BA:COL_BA + 16] = np.where((lane & 4) != 0, off_alpha, off_beta) + ((lane >> 3) & 1) * DN_HEADS + (lane & 3)
    src[COL_RW:COL_RW + RW_COLS] = off_rw + np.arange(RW_COLS)
    return src


def _bdot(a, b):
    return jnp.dot(a.astype(BF16), b.astype(BF16), preferred_element_type=F32)


def _bdot_nt(a, b):
    return lax.dot_general(a.astype(BF16), b.astype(BF16), (((1,), (1,)), ((), ())),
                           preferred_element_type=F32)


def _bdot_tn(a, b):
    return lax.dot_general(a.astype(BF16), b.astype(BF16), (((0,), (0,)), ((), ())),
                           preferred_element_type=F32)


def _fdot(a, b):
    return jnp.dot(a, b, precision=HI, preferred_element_type=F32)


def _sigmoid(x):
    return 1.0 / (1.0 + jnp.exp(-x))


def _softplus(x):
    return jnp.maximum(x, 0.0) + jnp.log(1.0 + jnp.exp(-jnp.abs(x)))


def _silu(x):
    return x * _sigmoid(x)


def _cparams(n_axes):
    return pltpu.CompilerParams(dimension_semantics=("arbitrary",) * n_axes, vmem_limit_bytes=VMEM_LIMIT)


def _unit_tri_inverse(m, n):
    ri = lax.broadcasted_iota(jnp.int32, (n, n), 0)
    ci = lax.broadcasted_iota(jnp.int32, (n, n), 1)
    eye = (ri == ci).astype(F32)
    t = eye - jnp.where((ri >> 1) == (ci >> 1), m, 0.0)
    for lg in range(2, 7):
        c = jnp.where(((ri >> lg) == (ci >> lg)) & ((ri >> (lg - 1)) != (ci >> (lg - 1))), m, 0.0)
        t = t - _bdot(t, _bdot(c, t))
    return t


def _norm_proj_kernel(x_ref, g_ref, w_ref, o_ref, xn_ref):
    @pl.when(pl.program_id(1) == 0)
    def _():
        x = x_ref[...]
        y = x * lax.rsqrt(jnp.mean(x * x, axis=-1, keepdims=True) + NORM_EPS)
        xn_ref[...] = (y * g_ref[...]).astype(BF16)

    o_ref[...] = jnp.dot(xn_ref[...], w_ref[...], preferred_element_type=F32)


def _norm_proj(x, g, w, tm, tn):
    n, d = x.shape
    cols = w.shape[1]
    return pl.pallas_call(
        _norm_proj_kernel,
        grid=(n // tm, cols // tn),
        in_specs=[pl.BlockSpec((tm, d), lambda i, j: (i, 0)),
                  pl.BlockSpec((1, d), lambda i, j: (0, 0)),
                  pl.BlockSpec((d, tn), lambda i, j: (0, j))],
        out_specs=pl.BlockSpec((tm, tn), lambda i, j: (i, j)),
        out_shape=jax.ShapeDtypeStruct((n, cols), F32),
        scratch_shapes=[pltpu.VMEM((tm, d), BF16)],
        compiler_params=_cparams(2),
        name="norm_proj",
    )(x, g.reshape(1, d), w)


def _rw_prep_kernel(x_ref, xp_ref, xn_ref, mu_ref, w0_ref, w2_ref, a0_ref, a2_ref, g2_ref, kk_w_ref, ka_ref,
                    rk_ref, ones_ref, r_ref, v_ref, kk_ref, lw_ref, kd_ref, b_ref, bonus_ref, gate_ref,
                    *, tiles_per_seq):
    i = pl.program_id(0)
    tm = x_ref.shape[0]
    x = x_ref[...]
    first = (i % tiles_per_seq) == 0
    last = (i % tiles_per_seq) == tiles_per_seq - 1
    row = lax.broadcasted_iota(jnp.int32, x.shape, 0)
    prev_edge = jnp.where(first, 0.0, xp_ref[7:8, :])
    next_edge = jnp.where(last, 0.0, xn_ref[0:1, :])
    prev = jnp.where(row == 0, prev_edge, pltpu.roll(x, 1, axis=0))
    nxt = jnp.where(row == tm - 1, next_edge, pltpu.roll(x, tm - 1, axis=0))
    f = x + (0.5 * (prev + nxt) - x) * mu_ref[...]
    w = RW_WIDTH
    r, k, v = f[:, 0:w], f[:, w:2 * w], f[:, 2 * w:3 * w]
    w_lo, a_lo, g_lo = f[:, 3 * w:3 * w + 128], f[:, 3 * w + 128:3 * w + 256], f[:, 3 * w + 256:3 * w + 384]
    ones_bd = ones_ref[...]
    kq = k * kk_w_ref[...]
    kk = kq * lax.rsqrt(_fdot(kq * kq, ones_bd) + NORM_EPS)
    gate_ref[...] = _bdot(_sigmoid(g_lo), g2_ref[...])
    w_log = -_softplus(-(w0_ref[...] + _bdot(jnp.tanh(w_lo), w2_ref[...]))) - 0.5
    lw_ref[...] = -jnp.exp(w_log)
    a_lr = _sigmoid(a0_ref[...] + _bdot(a_lo, a2_ref[...]))
    ka = ka_ref[...]
    kd0 = k * (1.0 + (a_lr[:, 0:w] - 1.0) * ka)
    kd1 = k * (1.0 + (a_lr[:, w:2 * w] - 1.0) * ka)
    kd_ref[:, 0:w] = kd0
    kd_ref[:, w:2 * w] = kd1
    b_ref[:, 0:w] = kk * a_lr[:, 0:w]
    b_ref[:, w:2 * w] = kk * a_lr[:, w:2 * w]
    bonus_ref[...] = _fdot(r * (kd0 + kd1) * rk_ref[...], ones_bd) * v
    r_ref[...] = r
    v_ref[...] = v
    kk_ref[...] = kk


def _rw_prep(proj, p, seqlen, tm):
    n = proj.shape[0]
    w = RW_WIDTH
    tps = seqlen // tm
    nb8 = n // 8
    full = lambda a: pl.BlockSpec(a.shape, lambda i: (0,) * a.ndim)
    consts = [p['mu'], p['w0'], p['w2'], p['a0'], p['a2'], p['g2'], p['k_k'], p['k_a'], p['r_k'], p['ones']]
    out_w = [w, w, w, 2 * w, 2 * w, 2 * w, w, w]
    return pl.pallas_call(
        functools.partial(_rw_prep_kernel, tiles_per_seq=tps),
        grid=(n // tm,),
        in_specs=[pl.BlockSpec((tm, RW_PAD), lambda i: (i, COL_RW // RW_PAD)),
                  pl.BlockSpec((8, RW_PAD), lambda i: (jnp.maximum(i * (tm // 8) - 1, 0), COL_RW // RW_PAD)),
                  pl.BlockSpec((8, RW_PAD), lambda i: (jnp.minimum((i + 1) * (tm // 8), nb8 - 1),
                                                       COL_RW // RW_PAD))]
                 + [full(a) for a in consts],
        out_specs=[pl.BlockSpec((tm, c), lambda i: (i, 0)) for c in out_w],
        out_shape=[jax.ShapeDtypeStruct((n, c), F32) for c in out_w],
        compiler_params=_cparams(1),
        name="rw_prep",
    )(proj, proj, proj, *consts)


def _rw_scan_kernel(r_ref, v_ref, kk_ref, lw_ref, kd_ref, b_ref, y_ref, ht_ref):
    d = pl.program_id(0)
    fwd = d == 0

    @pl.when(pl.program_id(2) == 0)
    def _():
        ht_ref[...] = jnp.zeros_like(ht_ref)

    c, pw = CHUNK, 128
    ri = lax.broadcasted_iota(jnp.int32, (pw, pw), 0)
    ci = lax.broadcasted_iota(jnp.int32, (pw, pw), 1)
    ti, tj = ri & (c - 1), ci & (c - 1)
    same_head = (ri >> 6) == (ci >> 6)
    sgn = 1 - 2 * d
    before = (ti - tj) * sgn > 0
    strict = same_head & before
    incl = same_head & (before | (ti == tj))
    i64 = lax.broadcasted_iota(jnp.int32, (c, c), 0)
    j64 = lax.broadcasted_iota(jnp.int32, (c, c), 1)
    cum_mask = ((i64 - j64) * sgn >= 0).astype(F32)
    head0 = lax.broadcasted_iota(jnp.int32, (c, pw), 1) < RW_HEAD_DIM

    def stack(x):
        return jnp.concatenate([jnp.where(head0, x, 0.0), jnp.where(head0, 0.0, x)], axis=0)

    for p in range(RW_WIDTH // pw):
        sl = slice(pw * p, pw * (p + 1))
        lw = lw_ref[:, sl]
        cum = _fdot(cum_mask, lw)
        tot = jnp.where(fwd, cum[c - 1:c, :], cum[0:1, :])
        p_inv = jnp.exp(-cum)
        p_end_over = jnp.exp(tot - cum)
        kd, bb, kk = kd_ref[:, sl], b_ref[:, sl], kk_ref[:, sl]
        rd_s = stack(r_ref[:, sl] * jnp.exp(cum))
        kp_s = stack(kk * jnp.exp(cum - lw))
        kinv_s = stack(kd * p_inv)
        binv_s = stack(bb * p_inv)
        kend_s = stack(kd * p_end_over)
        bend_s = stack(bb * p_end_over)
        v_s = stack(v_ref[:, sl])
        kb_inv = jnp.concatenate([kinv_s, binv_s], axis=0)
        a_k = _bdot_nt(kp_s, kb_inv)
        a_r = _bdot_nt(rd_s, kb_inv)
        t_inv = _unit_tri_inverse(jnp.where(strict, a_k[:, pw:], 0.0), pw)
        ht = ht_ref[p]
        carry = _bdot_nt(jnp.concatenate([kp_s, rd_s], axis=0), ht)
        u_s = _bdot(t_inv, carry[:pw] + _bdot(jnp.where(strict, a_k[:, :pw], 0.0), v_s))
        a_out = jnp.concatenate([jnp.where(incl, a_r[:, :pw], 0.0), -jnp.where(incl, a_r[:, pw:], 0.0)], axis=1)
        y_s = carry[pw:] + _bdot(a_out, jnp.concatenate([v_s, u_s], axis=0))
        y_ref[:, sl] = y_s[:c] + y_s[c:]
        upd = _bdot_tn(jnp.concatenate([v_s, -u_s], axis=0), jnp.concatenate([kend_s, bend_s], axis=0))
        ht_ref[p] = ht * jnp.exp(tot) + upd


def _rw_scan(r, v, kk, lw, kd, b, bsz, seqlen):
    n, w = r.shape
    nc = seqlen // CHUNK
    rowblk = lambda d, bi, c: bi * nc + c + d * (nc - 1 - 2 * c)
    shared = pl.BlockSpec((CHUNK, w), lambda d, bi, c: (rowblk(d, bi, c), 0))
    per_dir = pl.BlockSpec((CHUNK, w), lambda d, bi, c: (rowblk(d, bi, c), d))
    return pl.pallas_call(
        _rw_scan_kernel,
        grid=(2, bsz, nc),
        in_specs=[shared, shared, shared, per_dir, per_dir, per_dir],
        out_specs=pl.BlockSpec((None, CHUNK, w), lambda d, bi, c: (d, rowblk(d, bi, c), 0)),
        out_shape=jax.ShapeDtypeStruct((2, n, w), F32),
        scratch_shapes=[pltpu.VMEM((w // 128, 128, 128), F32)],
        compiler_params=_cparams(3),
        name="rw_scan",
    )(r, v, kk, lw, kd, b)


def _rw_params(p, layer):
    w = RW_WIDTH
    pad = lambda a: jnp.pad(a, ((0, 0), (0, RW_PAD - a.shape[1])))
    blockdiag2 = lambda m: jnp.concatenate(
        [jnp.concatenate([m[0], jnp.zeros_like(m[0])], axis=1),
         jnp.concatenate([jnp.zeros_like(m[1]), m[1]], axis=1)], axis=0)
    hid = jnp.arange(w) // RW_HEAD_DIM
    return {
        'mu': pad(p['rw_mu'][layer][None, :]),
        'w0': p['rw_w0'][layer].reshape(1, 2 * w),
        'w2': blockdiag2(p['rw_w2'][layer]).astype(BF16),
        'a0': p['rw_a0'][layer].reshape(1, 2 * w),
        'a2': blockdiag2(p['rw_a2'][layer]).astype(BF16),
        'g2': p['rw_g2'][layer].astype(BF16),
        'k_k': p['rw_k_k'][layer][None, :],
        'k_a': p['rw_k_a'][layer][None, :],
        'r_k': p['rw_r_k'][layer].reshape(1, w),
        'ones': (hid[:, None] == hid[None, :]).astype(F32),
    }


def _dn_prep_kernel(x_ref, xp_ref, xn_ref, ba_ref, cw_ref, alog_ref, dtb_ref, ones_ref,
                    q_ref, k_ref, v_ref, bg_ref, *, tiles_per_seq):
    i = pl.program_id(0)
    tm = x_ref.shape[0]
    x = x_ref[...]
    first = (i % tiles_per_seq) == 0
    last = (i % tiles_per_seq) == tiles_per_seq - 1
    row = lax.broadcasted_iota(jnp.int32, x.shape, 0)
    xp = jnp.where(first, 0.0, xp_ref[...])
    xn = jnp.where(last, 0.0, xn_ref[...])
    pad = (DN_CONV - 1) // 2
    acc = x * cw_ref[pad:pad + 1, :]
    for s in range(1, pad + 1):
        back = pltpu.roll(x, s, axis=0)
        fore = pltpu.roll(x, tm - s, axis=0)
        for t in range(s):
            back = jnp.where(row == t, xp[8 - s + t:9 - s + t, :], back)
            fore = jnp.where(row == tm - 1 - t, xn[s - 1 - t:s - t, :], fore)
        acc = acc + back * cw_ref[pad - s:pad - s + 1, :] + fore * cw_ref[pad + s:pad + s + 1, :]
    y = _silu(acc)
    w = DN_WIDTH
    ones_bd = ones_ref[...]
    q, k = y[:, 0:w], y[:, w:2 * w]
    q_ref[...] = q * (lax.rsqrt(_fdot(q * q, ones_bd) + NORM_EPS) * (DN_HEAD_DIM ** -0.5))
    k_ref[...] = k * lax.rsqrt(_fdot(k * k, ones_bd) + NORM_EPS)
    v_ref[...] = y[:, 2 * w:3 * w]
    ba = ba_ref[...]
    lane = lax.broadcasted_iota(jnp.int32, ba.shape, 1)
    g = -jnp.exp(alog_ref[...]) * _softplus(ba + dtb_ref[...])
    bg_ref[...] = jnp.where((lane & 4) == 0, _sigmoid(ba), g)


def _dn_prep(proj, p, seqlen, tm):
    n = proj.shape[0]
    w = DN_WIDTH
    tps = seqlen // tm
    nb8 = n // 8
    full = lambda a: pl.BlockSpec(a.shape, lambda i: (0,) * a.ndim)
    consts = [p['conv_w'], p['a_log'], p['dt_bias'], p['ones']]
    cb = COL_QKV // (3 * w)
    return pl.pallas_call(
        functools.partial(_dn_prep_kernel, tiles_per_seq=tps),
        grid=(n // tm,),
        in_specs=[pl.BlockSpec((tm, 3 * w), lambda i: (i, cb)),
                  pl.BlockSpec((8, 3 * w), lambda i: (jnp.maximum(i * (tm // 8) - 1, 0), cb)),
                  pl.BlockSpec((8, 3 * w), lambda i: (jnp.minimum((i + 1) * (tm // 8), nb8 - 1), cb)),
                  pl.BlockSpec((tm, 128), lambda i: (i, COL_BA // 128))]
                 + [full(a) for a in consts],
        out_specs=[pl.BlockSpec((tm, c), lambda i: (i, 0)) for c in (w, w, w, 128)],
        out_shape=[jax.ShapeDtypeStruct((n, c), F32) for c in (w, w, w, 128)],
        compiler_params=_cparams(1),
        name="dn_prep",
    )(proj, proj, proj, proj, *consts)


def _dn_scan_kernel(q_ref, k_ref, v_ref, bg_ref, o_ref, s_ref):
    d = pl.program_id(0)
    fwd = d == 0

    @pl.when(pl.program_id(2) == 0)
    def _():
        s_ref[...] = jnp.zeros_like(s_ref)

    c, hd, nh = CHUNK, DN_HEAD_DIM, DN_HEADS
    n = nh * c
    ri = lax.broadcasted_iota(jnp.int32, (n, n), 0)
    ci = lax.broadcasted_iota(jnp.int32, (n, n), 1)
    ti, tj = ri & (c - 1), ci & (c - 1)
    same_head = (ri >> 6) == (ci >> 6)
    sgn = 1 - 2 * d
    before = (ti - tj) * sgn > 0
    strict = same_head & before
    incl = same_head & (before | (ti == tj))
    i64 = lax.broadcasted_iota(jnp.int32, (c, c), 0)
    j64 = lax.broadcasted_iota(jnp.int32, (c, c), 1)
    cum_mask = ((i64 - j64) * sgn >= 0).astype(F32)

    bg_all = bg_ref[...]
    gcum_all = _fdot(cum_mask, bg_all)
    bg = jnp.where(fwd, bg_all[:, 0:8], bg_all[:, 8:16])
    gcum = jnp.where(fwd, gcum_all[:, 0:8], gcum_all[:, 8:16])
    gtot = jnp.where(fwd, gcum[c - 1:c, :], gcum[0:1, :])
    beta_s = jnp.concatenate([bg[:, h:h + 1] for h in range(nh)], axis=0)
    gc_s = jnp.concatenate([gcum[:, nh + h:nh + h + 1] for h in range(nh)], axis=0)
    gend_s = jnp.concatenate([jnp.broadcast_to(gtot[:, nh + h:nh + h + 1], (c, 1)) for h in range(nh)], axis=0)
    stack = lambda ref: jnp.concatenate([ref[:, hd * h:hd * (h + 1)] for h in range(nh)], axis=0)
    q_s, k_s, v_s = stack(q_ref), stack(k_ref), stack(v_ref)

    gc_rows = jnp.broadcast_to(gc_s, (n, n))
    dlog = gc_rows - gc_rows.T
    decay = jnp.exp(jnp.where(incl, dlog, -1e30))
    kb_s = k_s * beta_s
    qk = _bdot_nt(jnp.concatenate([kb_s, q_s], axis=0), k_s)
    t_inv = _unit_tri_inverse(jnp.where(strict, qk[:n] * decay, 0.0), n)
    a_qk = jnp.where(incl, qk[n:] * decay, 0.0)
    e_gc = jnp.exp(gc_s)
    uw = _bdot(t_inv, jnp.concatenate([v_s * beta_s, kb_s * e_gc], axis=1))
    q_dec = q_s * e_gc
    k_dec = k_s * jnp.exp(gend_s - gc_s)
    e_end = jnp.exp(gtot)
    carry = []
    for h in range(nh):
        rows = slice(c * h, c * (h + 1))
        carry.append(_bdot(jnp.concatenate([uw[rows, hd:], q_dec[rows]], axis=0), s_ref[h]))
    v_new = uw[:, :hd] - jnp.concatenate([cr[:c] for cr in carry], axis=0)
    o_s = jnp.concatenate([cr[c:] for cr in carry], axis=0) + _bdot(a_qk, v_new)
    for h in range(nh):
        rows = slice(c * h, c * (h + 1))
        o_ref[:, hd * h:hd * (h + 1)] = o_s[rows]
        s_ref[h] = s_ref[h] * e_end[:, nh + h:nh + h + 1] + _bdot_tn(k_dec[rows], v_new[rows])


def _dn_scan(q, k, v, bg, bsz, seqlen):
    n, w = q.shape
    nc = seqlen // CHUNK
    rowblk = lambda d, bi, c: bi * nc + c + d * (nc - 1 - 2 * c)
    spec = lambda cols: pl.BlockSpec((CHUNK, cols), lambda d, bi, c: (rowblk(d, bi, c), 0))
    return pl.pallas_call(
        _dn_scan_kernel,
        grid=(2, bsz, nc),
        in_specs=[spec(w), spec(w), spec(w), spec(128)],
        out_specs=pl.BlockSpec((None, CHUNK, w), lambda d, bi, c: (d, rowblk(d, bi, c), 0)),
        out_shape=jax.ShapeDtypeStruct((2, n, w), F32),
        scratch_shapes=[pltpu.VMEM((DN_HEADS, DN_HEAD_DIM, DN_HEAD_DIM), F32)],
        compiler_params=_cparams(3),
        name="dn_scan",
    )(q, k, v, bg)


def _dn_params(p, layer):
    lane = jnp.arange(128)
    dirn, hh, is_alpha = (lane >> 3) & 1, lane & 3, ((lane & 4) != 0) & (lane < 16)
    a_log = jnp.where(is_alpha, p['dn_a_log'][layer][dirn, hh], 0.0)
    dt_bias = jnp.where(is_alpha, p['dn_dt_bias'][layer][dirn, hh], 0.0)
    hid = jnp.arange(DN_WIDTH) // DN_HEAD_DIM
    return {
        'conv_w': jnp.pad(p['dn_conv_w'][layer], ((0, 8 - DN_CONV), (0, 0))),
        'a_log': a_log[None, :].astype(F32),
        'dt_bias': dt_bias[None, :].astype(F32),
        'ones': (hid[:, None] == hid[None, :]).astype(F32),
    }


S5_CHUNK = 16
S5_FLAT = S5_CHUNK * S5_GROUP


def _s5_kernel(u_ref, ws_ref, wy_ref, ca_ref, cb_ref, y_ref, *, levels):
    u = u_ref[...]
    nc = u.shape[0]
    x = _bdot(u, ws_ref[...])
    row = lax.broadcasted_iota(jnp.int32, (nc, 128), 0)
    ca, cb = ca_ref[...], cb_ref[...]

    def cmul(k, half, h):
        a = ca[k:k + 1, 128 * half:128 * (half + 1)]
        b = cb[k:k + 1, 128 * half:128 * (half + 1)]
        return a * h + b * pltpu.roll(h, S5_STATE, axis=1)

    hf, hb = x[:, 0:128], x[:, 128:256]
    for k in range(levels):
        s = 1 << k
        hf = hf + jnp.where(row >= s, cmul(k, 0, pltpu.roll(hf, s, axis=0)), 0.0)
        hb = hb + jnp.where(row < nc - s, cmul(k, 1, pltpu.roll(hb, nc - s, axis=0)), 0.0)
    hf = jnp.where(row >= 1, pltpu.roll(hf, 1, axis=0), 0.0)
    hb = jnp.where(row < nc - 1, pltpu.roll(hb, nc - 1, axis=0), 0.0)
    y_ref[...] = _bdot(jnp.concatenate([u, hf, hb], axis=1), wy_ref[...])


def _s5_core(proj, sp, bsz, seqlen):
    n = proj.shape[0]
    nc = seqlen // S5_CHUNK
    u = lax.slice_in_dim(proj, COL_S5, COL_S5 + S5_WIDTH, axis=1)
    ug = u.reshape(bsz, nc, S5_CHUNK, S5_GROUPS, S5_GROUP).transpose(3, 0, 1, 2, 4).reshape(S5_GROUPS, bsz * nc, S5_FLAT)
    levels = sp['ca'].shape[1]
    wspec = lambda a: pl.BlockSpec((None,) + a.shape[1:], lambda g, b: (g, 0, 0))
    yg = pl.pallas_call(
        functools.partial(_s5_kernel, levels=levels),
        grid=(S5_GROUPS, bsz),
        in_specs=[pl.BlockSpec((None, nc, S5_FLAT), lambda g, b: (g, b, 0)),
                  wspec(sp['ws']), wspec(sp['wy']), wspec(sp['ca']), wspec(sp['cb'])],
        out_specs=pl.BlockSpec((None, nc, S5_FLAT), lambda g, b: (g, b, 0)),
        out_shape=jax.ShapeDtypeStruct((S5_GROUPS, bsz * nc, S5_FLAT), F32),
        compiler_params=_cparams(2),
        name="s5_core",
    )(ug, sp['ws'], sp['wy'], sp['ca'], sp['cb'])
    return yg.reshape(S5_GROUPS, bsz, nc, S5_CHUNK, S5_GROUP).transpose(1, 2, 3, 0, 4).reshape(n, S5_WIDTH)


def _s5_params(p, layer, seqlen):
    cs, gs, ps = S5_CHUNK, S5_GROUP, S5_STATE
    cexp = lambda zr, zi: (jnp.exp(zr) * jnp.cos(zi), jnp.exp(zr) * jnp.sin(zi))
    cmul = lambda a, b: (a[0] * b[0] - a[1] * b[1], a[0] * b[1] + a[1] * b[0])
    lam_re, lam_im = p['s5_lam_re'][layer].astype(F32), p['s5_lam_im'][layer].astype(F32)
    step = jnp.exp(p['s5_log_step'][layer].astype(F32))[..., None]
    zr, zi = lam_re * step, lam_im * step
    lbar_re, lbar_im = cexp(zr, zi)
    den = lam_re * lam_re + lam_im * lam_im
    num_re = lbar_re - 1.0
    f = ((num_re * lam_re + lbar_im * lam_im) / den, (lbar_im * lam_re - num_re * lam_im) / den)
    bbar = cmul((f[0][..., None], f[1][..., None]),
                (p['s5_b_re'][layer].astype(F32), p['s5_b_im'][layer].astype(F32)))
    cc = (p['s5_c_re'][layer].astype(F32), p['s5_c_im'][layer].astype(F32))
    tau = jnp.arange(cs + 1, dtype=F32)
    pw = cexp(zr[..., None] * tau, zi[..., None] * tau)
    m = cmul((pw[0][..., None], pw[1][..., None]), (bbar[0][..., None, :], bbar[1][..., None, :]))
    kern = (jnp.einsum('dgcp,dgpte->dgtce', cc[0], m[0], precision=HI)
            - jnp.einsum('dgcp,dgpte->dgtce', cc[1], m[1], precision=HI))
    s_i, t_i = jnp.arange(cs)[:, None], jnp.arange(cs)[None, :]
    kf = jnp.where((t_i >= s_i)[None, :, :, None, None], kern[0][:, jnp.clip(t_i - s_i, 0, cs)], 0.0)
    kb = jnp.where((s_i >= t_i)[None, :, :, None, None], kern[1][:, jnp.clip(s_i - t_i, 0, cs)], 0.0)
    tmat = (kf + kb).transpose(0, 1, 4, 2, 3).reshape(S5_GROUPS, cs * gs, cs * gs)
    sidx = jnp.arange(cs)
    sel = lambda d, idx: (pw[0][d][..., idx][..., None], pw[1][d][..., idx][..., None])
    inj = lambda d: (bbar[0][d][:, :, None, :], bbar[1][d][:, :, None, :])
    ws_f = cmul(sel(0, cs - 1 - sidx), inj(0))
    ws_b = cmul(sel(1, sidx), inj(1))
    to_cols = lambda w: jnp.concatenate([w[0], w[1]], axis=1).transpose(0, 2, 3, 1).reshape(
        S5_GROUPS, cs * gs, 2 * ps)
    ws = jnp.concatenate([to_cols(ws_f), to_cols(ws_b)], axis=2)
    out = lambda d: (cc[0][d].transpose(0, 2, 1)[:, :, None, :], cc[1][d].transpose(0, 2, 1)[:, :, None, :])
    wo_f = cmul(out(0), sel(0, 1 + sidx))
    wo_b = cmul(out(1), sel(1, cs - sidx))
    to_rows = lambda w: jnp.concatenate([w[0], -w[1]], axis=1).reshape(S5_GROUPS, 2 * ps, cs * gs)
    wy = jnp.concatenate([tmat, to_rows(wo_f), to_rows(wo_b)], axis=1)
    levels = max(1, (seqlen // cs - 1).bit_length())
    span = cs * (2.0 ** jnp.arange(levels, dtype=F32))
    lk = cexp(zr[..., None] * span, zi[..., None] * span)
    re, im = lk[0].transpose(0, 1, 3, 2), lk[1].transpose(0, 1, 3, 2)
    ca = jnp.concatenate([re[0], re[0], re[1], re[1]], axis=-1)
    cb = jnp.concatenate([-im[0], im[0], -im[1], im[1]], axis=-1)
    return {'ws': ws.astype(BF16), 'wy': wy.astype(BF16), 'ca': ca.astype(F32), 'cb': cb.astype(F32)}


def _group_sum(x, ones_bf16):
    hi = x.astype(BF16)
    lo = (x - hi.astype(F32)).astype(BF16)
    return (jnp.dot(hi, ones_bf16, preferred_element_type=F32) + jnp.dot(lo, ones_bf16, preferred_element_type=F32))


def _gelu_tanh(x):
    return 0.5 * x * (1.0 + jnp.tanh(math.sqrt(2.0 / math.pi) * (x + 0.044715 * (x * x * x))))


def _merge_kernel(h_ref, gate_ref, z_ref, u_ref, odn_ref, ys5_ref, yrw_ref, bonus_ref, grw_ref,
                  ones128_ref, ones64_ref, dng_ref, s5d_ref, glub_ref, lng_ref, lnb_ref,
                  gluw_ref, dnp_ref, s5p_ref, rwp_ref, wout_ref, o_ref):
    o = odn_ref[0] + odn_ref[1]
    ms = _group_sum(o * o, ones128_ref[...]) * (1.0 / DN_HEAD_DIM)
    o_a = o * lax.rsqrt(ms + NORM_EPS) * dng_ref[...] * _silu(z_ref[...])
    y = _gelu_tanh(u_ref[...] * s5d_ref[...] + ys5_ref[...])
    o_b = y * _sigmoid(_bdot(y, gluw_ref[...]) + glub_ref[...])
    yr = yrw_ref[0] + yrw_ref[1]
    ones64 = ones64_ref[...]
    cen = yr - _group_sum(yr, ones64) * (1.0 / RW_HEAD_DIM)
    var = _group_sum(cen * cen, ones64) * (1.0 / RW_HEAD_DIM)
    o_c = (cen * lax.rsqrt(var + RW_GN_EPS) * lng_ref[...] + lnb_ref[...] + bonus_ref[...]) * grw_ref[...]
    dm = D_MODEL
    merged = (_sigmoid(gate_ref[:, 0:dm]) * _bdot(o_a, dnp_ref[...])
              + _sigmoid(gate_ref[:, dm:2 * dm]) * _bdot(o_b, s5p_ref[...])
              + _sigmoid(gate_ref[:, 2 * dm:3 * dm]) * _bdot(o_c, rwp_ref[...]))
    o_ref[...] = h_ref[...] + _bdot(merged, wout_ref[...])


def _merge(h, proj, o_dn, y_s5, y_rw, bonus, gate_rw, mp, tm):
    n, dm = h.shape
    w = DN_WIDTH
    row = lambda cols, cb=0: pl.BlockSpec((tm, cols), lambda i: (i, cb))
    pair = pl.BlockSpec((2, tm, w), lambda i: (0, i, 0))
    full = lambda a: pl.BlockSpec(a.shape, lambda i: (0,) * a.ndim)
    consts = [mp['ones128'], mp['ones64'], mp['dn_norm_g'], mp['s5_d'], mp['glu_b'], mp['ln_g'], mp['ln_b'],
              mp['glu_w'], mp['dn_proj'], mp['s5_proj'], mp['rw_proj'], mp['w_out']]
    return pl.pallas_call(
        _merge_kernel,
        grid=(n // tm,),
        in_specs=[row(dm), row(3 * dm, COL_GATE // (3 * dm)), row(w, COL_Z // w), row(w, COL_S5 // w),
                  pair, row(w), pair, row(w), row(w)] + [full(a) for a in consts],
        out_specs=row(dm),
        out_shape=jax.ShapeDtypeStruct((n, dm), F32),
        compiler_params=_cparams(1),
        name="merge",
    )(h, proj, proj, proj, o_dn, y_s5, y_rw, bonus, gate_rw, *consts)


def _merge_params(p, layer):
    hid128 = jnp.arange(DN_WIDTH) // DN_HEAD_DIM
    hid64 = jnp.arange(RW_WIDTH) // RW_HEAD_DIM
    r1 = lambda a: a.reshape(1, -1).astype(F32)
    return {
        'ones128': (hid128[:, None] == hid128[None, :]).astype(BF16),
        'ones64': (hid64[:, None] == hid64[None, :]).astype(BF16),
        'dn_norm_g': r1(jnp.tile(p['dn_norm_g'][layer], DN_HEADS)),
        's5_d': r1(p['s5_d'][layer]), 'glu_b': r1(p['s5_glu_b'][layer]),
        'ln_g': r1(p['rw_ln_g'][layer]), 'ln_b': r1(p['rw_ln_b'][layer]),
        'glu_w': p['s5_glu_w'][layer].astype(BF16), 'dn_proj': p['dn_proj'][layer].astype(BF16),
        's5_proj': p['s5_proj'][layer].astype(BF16), 'rw_proj': p['rw_proj'][layer].astype(BF16),
        'w_out': p['w_out'][layer].astype(BF16),
    }


MOE_TILE = 256
MOE_WIN = MOE_TILE + 16
MOE_ROWS = 512


def _slot_rows(cap):
    tr = min(MOE_ROWS, cap)
    return -(-(cap + MOE_WIN) // tr) * tr


def _router_kernel(h_ref, g_ref, rw_ref, hn_ref, aff_ref):
    x = h_ref[...]
    hn = (x * lax.rsqrt(jnp.mean(x * x, axis=-1, keepdims=True) + NORM_EPS) * g_ref[...]).astype(BF16)
    hn_ref[...] = hn
    logits = lax.dot_general(rw_ref[...], hn, (((1,), (1,)), ((), ())), preferred_element_type=F32)
    ex = jnp.exp(logits - jnp.max(logits, axis=0, keepdims=True))
    aff_ref[...] = ex / jnp.sum(ex, axis=0, keepdims=True)


def _router(h, g, router_wt, tm):
    n, dm = h.shape
    return pl.pallas_call(
        _router_kernel,
        grid=(n // tm,),
        in_specs=[pl.BlockSpec((tm, dm), lambda i: (i, 0)), pl.BlockSpec((1, dm), lambda i: (0, 0)),
                  pl.BlockSpec((N_EXPERTS, dm), lambda i: (0, 0))],
        out_specs=[pl.BlockSpec((tm, dm), lambda i: (i, 0)), pl.BlockSpec((N_EXPERTS, tm), lambda i: (0, i))],
        out_shape=[jax.ShapeDtypeStruct((n, dm), BF16), jax.ShapeDtypeStruct((N_EXPERTS, n), F32)],
        compiler_params=_cparams(1),
        name="router",
    )(h, g.reshape(1, dm), router_wt)


def _threshold_kernel(aff_ref, thr_ref, *, cap):
    bits = pltpu.bitcast(aff_ref[...], jnp.int32)

    def body(i, t):
        cand = t | (jnp.int32(1) << (30 - i))
        cnt = jnp.sum((bits >= cand).astype(F32), axis=1, keepdims=True)
        return jnp.where(cnt >= cap, cand, t)

    t = lax.fori_loop(0, 31, body, jnp.zeros((N_EXPERTS, 1), jnp.int32))
    thr_ref[...] = jnp.broadcast_to(t, thr_ref.shape)


def _threshold(aff_t, cap):
    e, n = aff_t.shape
    return pl.pallas_call(
        functools.partial(_threshold_kernel, cap=cap),
        grid=(1,),
        in_specs=[pl.BlockSpec((e, n), lambda i: (0, 0))],
        out_specs=pl.BlockSpec((e, 128), lambda i: (0, 0)),
        out_shape=jax.ShapeDtypeStruct((e, 128), jnp.int32),
        compiler_params=_cparams(1),
        name="moe_threshold",
    )(aff_t)


def _slots_kernel(thr_ref, aff_ref, slot_ref, posx_ref, *, cap):
    e = pl.program_id(0)
    thr = thr_ref[e]
    bits = pltpu.bitcast(aff_ref[...], jnp.int32)
    nb = bits.shape[0]
    li = lax.broadcasted_iota(jnp.int32, (128, 128), 0)
    lj = lax.broadcasted_iota(jnp.int32, (128, 128), 1)
    upper = (li <= lj).astype(BF16)
    bi = lax.broadcasted_iota(jnp.int32, (nb, nb), 0)
    bj = lax.broadcasted_iota(jnp.int32, (nb, nb), 1)
    lower_strict = (bj < bi).astype(BF16)

    def exclusive_count(m):
        incl = jnp.dot(m.astype(BF16), upper, preferred_element_type=F32)
        tot = jnp.broadcast_to(incl[:, 127:128], (nb, 128))
        offs = jnp.dot(lower_strict, tot.astype(BF16), preferred_element_type=F32)
        return offs + incl - m

    gt = (bits > thr).astype(F32)
    eq = (bits == thr).astype(F32)
    need = cap - jnp.sum(jnp.sum(gt, axis=1, keepdims=True), axis=0, keepdims=True)
    sel = jnp.maximum(gt, eq * (exclusive_count(eq) < need).astype(F32))
    pos = exclusive_count(sel).astype(jnp.int32)
    posx_ref[...] = pos
    slot_ref[...] = jnp.where(sel > 0.0, pos, -1)


def _slots(aff_t, thr, cap):
    e, n = aff_t.shape
    nb = n // 128
    blk = pl.BlockSpec((nb, 128), lambda i, thr_ref: (i, 0))
    slot, posx = pl.pallas_call(
        functools.partial(_slots_kernel, cap=cap),
        grid_spec=pltpu.PrefetchScalarGridSpec(
            num_scalar_prefetch=1, grid=(e,), in_specs=[blk], out_specs=[blk, blk]),
        out_shape=[jax.ShapeDtypeStruct((e * nb, 128), jnp.int32)] * 2,
        compiler_params=_cparams(1),
        name="moe_slots",
    )(thr[:, 0], aff_t.reshape(e * nb, 128))
    return slot.reshape(e, n), posx.reshape(e, n)


def _gather_kernel(base_ref, slot_ref, hn_ref, xs_ref):
    e, k = pl.program_id(0), pl.program_id(1)

    @pl.when(k == 0)
    def _():
        xs_ref[...] = jnp.zeros_like(xs_ref)

    start = pl.multiple_of((base_ref[e * pl.num_programs(1) + k] >> 4) << 4, 16)
    srow = lax.broadcasted_iota(jnp.int32, (MOE_WIN, MOE_TILE), 0)
    onehot = (slot_ref[...] - start == srow).astype(BF16)
    rows = jnp.dot(onehot, hn_ref[...], preferred_element_type=F32).astype(BF16)
    win = pl.ds(start, MOE_WIN)
    xs_ref[win, :] = xs_ref[win, :] + rows


def _gather(hn, slot, base, cap):
    n, dm = hn.shape
    nt = n // MOE_TILE
    rows = _slot_rows(cap)
    return pl.pallas_call(
        _gather_kernel,
        grid_spec=pltpu.PrefetchScalarGridSpec(
            num_scalar_prefetch=1, grid=(N_EXPERTS, nt),
            in_specs=[pl.BlockSpec((None, 1, MOE_TILE), lambda e, k, b: (e * nt + k, 0, 0)),
                      pl.BlockSpec((MOE_TILE, dm), lambda e, k, b: (k, 0))],
            out_specs=pl.BlockSpec((None, rows, dm), lambda e, k, b: (e, 0, 0))),
        out_shape=jax.ShapeDtypeStruct((N_EXPERTS, rows, dm), BF16),
        compiler_params=_cparams(2),
        name="moe_gather",
    )(base, slot.reshape(N_EXPERTS * nt, 1, MOE_TILE), hn)


def _ffn_kernel(x_ref, w1_ref, w3_ref, w2_ref, o_ref):
    x = x_ref[...]
    a = jnp.dot(x, w1_ref[...], preferred_element_type=F32)
    g = jnp.dot(x, w3_ref[...], preferred_element_type=F32)
    o_ref[...] = jnp.dot((_silu(a) * g).astype(BF16), w2_ref[...], preferred_element_type=F32).astype(BF16)


def _ffn(xs, w1, w3, w2, cap):
    e, rows, dm = xs.shape
    ff = w1.shape[2]
    tr = min(MOE_ROWS, cap)
    return pl.pallas_call(
        _ffn_kernel,
        grid=(e, rows // tr),
        in_specs=[pl.BlockSpec((None, tr, dm), lambda i, r: (i, r, 0)),
                  pl.BlockSpec((None, dm, ff), lambda i, r: (i, 0, 0)),
                  pl.BlockSpec((None, dm, ff), lambda i, r: (i, 0, 0)),
                  pl.BlockSpec((None, ff, dm), lambda i, r: (i, 0, 0))],
        out_specs=pl.BlockSpec((None, tr, dm), lambda i, r: (i, r, 0)),
        out_shape=jax.ShapeDtypeStruct((e, rows, dm), BF16),
        compiler_params=_cparams(2),
        name="moe_ffn",
    )(xs, w1, w3, w2)


def _combine_kernel(base_ref, h_ref, slot_ref, gate_ref, g_ref, *rest, final_norm):
    wins, o_ref = rest[:N_EXPERTS], rest[N_EXPERTS]
    k = pl.program_id(0)
    nt = pl.num_programs(0)
    scol = lax.broadcasted_iota(jnp.int32, (MOE_TILE, MOE_WIN), 1)
    slot, gate = slot_ref[...], gate_ref[...]
    acc = h_ref[...]
    for e in range(N_EXPERTS):
        start = (base_ref[e * nt + k] >> 4) << 4
        onehot = (slot[:, e:e + 1] - start == scol).astype(BF16)
        acc = acc + jnp.dot(onehot, wins[e][...], preferred_element_type=F32) * gate[:, e:e + 1]
    if final_norm:
        acc = acc * lax.rsqrt(jnp.mean(acc * acc, axis=-1, keepdims=True) + NORM_EPS) * g_ref[...]
    o_ref[...] = acc


def _combine(h, slot_tm, gate_tm, base, outs, final_g, cap):
    n, dm = h.shape
    nt = n // MOE_TILE
    rows = outs.shape[1]
    outs = outs.reshape(N_EXPERTS * rows, dm)
    tile = lambda cols: pl.BlockSpec((MOE_TILE, cols), lambda k, b: (k, 0))

    def window(e):
        return pl.BlockSpec((pl.Element(MOE_WIN), pl.Element(dm)),
                            lambda k, b: (pl.multiple_of(e * rows + ((b[e * nt + k] >> 4) << 4), 16), 0))

    g = jnp.ones((1, dm), F32) if final_g is None else final_g.reshape(1, dm).astype(F32)
    return pl.pallas_call(
        functools.partial(_combine_kernel, final_norm=final_g is not None),
        grid_spec=pltpu.PrefetchScalarGridSpec(
            num_scalar_prefetch=1, grid=(nt,),
            in_specs=[tile(dm), tile(N_EXPERTS), tile(N_EXPERTS), pl.BlockSpec((1, dm), lambda k, b: (0, 0))]
                     + [window(e) for e in range(N_EXPERTS)],
            out_specs=tile(dm)),
        out_shape=jax.ShapeDtypeStruct((n, dm), F32),
        compiler_params=_cparams(1),
        name="moe_combine",
    )(base, h, slot_tm, gate_tm, g, *([outs] * N_EXPERTS))


def _moe(h, norm_g, router_wt, w1, w3, w2, final_g):
    n = h.shape[0]
    cap = CAPACITY_FACTOR * n // N_EXPERTS
    nt = n // MOE_TILE
    hn, aff_t = _router(h, norm_g, router_wt, 512)
    thr = _threshold(aff_t, cap)
    slot, posx = _slots(aff_t, thr, cap)
    base = posx[:, ::MOE_TILE].reshape(N_EXPERTS * nt)
    xs = _gather(hn, slot, base, cap)
    outs = _ffn(xs, w1, w3, w2, cap)
    return _combine(h, slot.T, aff_t.T, base, outs, final_g, cap)


def kernel(x_prompt, x_sample, norm1_g, norm2_g, final_norm_g, w_in, dn_conv_w, dn_a_log, dn_dt_bias, dn_norm_g, dn_proj, s5_lam_re, s5_lam_im, s5_log_step, s5_b_re, s5_b_im, s5_c_re, s5_c_im, s5_d, s5_glu_w, s5_glu_b, s5_proj, rw_mu, rw_w0, rw_w2, rw_a0, rw_a2, rw_g2, rw_k_k, rw_k_a, rw_r_k, rw_ln_g, rw_ln_b, rw_proj, w_out, router_w, expert_w1, expert_w3, expert_w2):
    p = dict(norm1_g=norm1_g, norm2_g=norm2_g, final_norm_g=final_norm_g, w_in=w_in, dn_conv_w=dn_conv_w,
             dn_a_log=dn_a_log, dn_dt_bias=dn_dt_bias, dn_norm_g=dn_norm_g, dn_proj=dn_proj, s5_lam_re=s5_lam_re,
             s5_lam_im=s5_lam_im, s5_log_step=s5_log_step, s5_b_re=s5_b_re, s5_b_im=s5_b_im, s5_c_re=s5_c_re,
             s5_c_im=s5_c_im, s5_d=s5_d, s5_glu_w=s5_glu_w, s5_glu_b=s5_glu_b, s5_proj=s5_proj, rw_mu=rw_mu,
             rw_w0=rw_w0, rw_w2=rw_w2, rw_a0=rw_a0, rw_a2=rw_a2, rw_g2=rw_g2, rw_k_k=rw_k_k, rw_k_a=rw_k_a,
             rw_r_k=rw_r_k, rw_ln_g=rw_ln_g, rw_ln_b=rw_ln_b, rw_proj=rw_proj, w_out=w_out, router_w=router_w,
             expert_w1=expert_w1, expert_w3=expert_w3, expert_w2=expert_w2)
    lp = _layer_params(p, (x_prompt.shape[1], x_sample.shape[1]))
    return _trunk(x_prompt, lp), _trunk(x_sample, lp)


def _layer_params(p, seqlens):
    depth = p['w_in'].shape[0]
    src = _proj_source_columns()
    layers = []
    for layer in range(depth):
        w_in = jnp.where(src[None, :] >= 0, p['w_in'][layer][:, src.clip(0)], 0.0).astype(BF16)
        layers.append({
            'norm1_g': p['norm1_g'][layer], 'norm2_g': p['norm2_g'][layer], 'w_in': w_in,
            'dn': _dn_params(p, layer), 'rw': _rw_params(p, layer), 'merge': _merge_params(p, layer),
            's5': {s: _s5_params(p, layer, s) for s in set(seqlens)},
            'router_wt': p['router_w'][layer].T.astype(BF16),
            'w1': p['expert_w1'][layer].astype(BF16), 'w3': p['expert_w3'][layer].astype(BF16),
            'w2': p['expert_w2'][layer].astype(BF16),
        })
    return {'layers': layers, 'final_norm_g': p['final_norm_g']}


def _trunk(x, lp):
    bsz, seqlen, dm = x.shape
    h = x.reshape(bsz * seqlen, dm)
    tm = min(256, seqlen)
    depth = len(lp['layers'])
    for li, w in enumerate(lp['layers']):
        proj = _norm_proj(h, w['norm1_g'], w['w_in'], min(512, seqlen), 1024)
        q, k, v, bg = _dn_prep(proj, w['dn'], seqlen, tm)
        o_dn = _dn_scan(q, k, v, bg, bsz, seqlen)
        y_s5 = _s5_core(proj, w['s5'][seqlen], bsz, seqlen)
        r, rv, kk, lw, kd, b, bonus, gate_rw = _rw_prep(proj, w['rw'], seqlen, tm)
        y_rw = _rw_scan(r, rv, kk, lw, kd, b, bsz, seqlen)
        h = _merge(h, proj, o_dn, y_s5, y_rw, bonus, gate_rw, w['merge'], tm)
        h = _moe(h, w['norm2_g'], w['router_wt'], w['w1'], w['w3'], w['w2'],
                 lp['final_norm_g'] if li == depth - 1 else None)
    return h.reshape(bsz, seqlen, dm)
```

```python
import functools
import math

import jax
import jax.numpy as jnp
from jax import lax
from jax.experimental import pallas as pl
from jax.experimental.pallas import tpu as pltpu

F32 = jnp.float32
BF16 = jnp.bfloat16
HI = lax.Precision.HIGHEST

D_MODEL = 1024
NORM_EPS = 1e-6
CHUNK = 64
DN_HEADS, DN_HEAD_DIM, DN_WIDTH, DN_CONV = 4, 128, 512, 5
S5_WIDTH, S5_GROUP, S5_GROUPS, S5_STATE = 512, 16, 32, 64
RW_WIDTH, RW_HEAD_DIM, RW_HEADS = 512, 64, 8
RW_DECAY_RANK, RW_AAA_RANK, RW_GATE_RANK = 64, 64, 128
RW_GN_EPS = 64e-5
RW_COLS = 3 * RW_WIDTH + 2 * RW_DECAY_RANK + 2 * RW_AAA_RANK + RW_GATE_RANK
N_EXPERTS, EXPERT_FF, CAPACITY_FACTOR = 16, 2048, 2
VMEM_LIMIT = 56 * 1024 * 1024

COL_GATE, COL_QKV, COL_Z, COL_S5, COL_BA, COL_RW, PROJ_COLS = 0, 3072, 4608, 5120, 5632, 6144, 8192
RW_PAD = 2048
HALO = 16


def _proj_source_columns():
    import numpy as np
    w = DN_WIDTH
    off_q, off_z, off_beta = 0, 3 * w, 4 * w
    off_alpha = off_beta + 2 * DN_HEADS
    off_s5 = off_alpha + 2 * DN_HEADS
    off_rw = off_s5 + S5_WIDTH
    off_gate = off_rw + RW_COLS
    src = np.full((PROJ_COLS,), -1, np.int32)
    src[COL_GATE:COL_GATE + 3 * D_MODEL] = off_gate + np.arange(3 * D_MODEL)
    src[COL_QKV:COL_QKV + 3 * w] = off_q + np.arange(3 * w)
    src[COL_Z:COL_Z + w] = off_z + np.arange(w)
    src[COL_S5:COL_S5 + S5_WIDTH] = off_s5 + np.arange(S5_WIDTH)
    lane = np.arange(16)
    src[COL_BA:COL_BA + 16] = np.where((lane & 4) != 0, off_alpha, off_beta) + ((lane >> 3) & 1) * DN_HEADS + (lane & 3)
    src[COL_RW:COL_RW + RW_COLS] = off_rw + np.arange(RW_COLS)
    return src


def _bdot(a, b):
    return jnp.dot(a.astype(BF16), b.astype(BF16), preferred_element_type=F32)


def _bdot_nt(a, b):
    return lax.dot_general(a.astype(BF16), b.astype(BF16), (((1,), (1,)), ((), ())),
                           preferred_element_type=F32)


def _bdot_tn(a, b):
    return lax.dot_general(a.astype(BF16), b.astype(BF16), (((0,), (0,)), ((), ())),
                           preferred_element_type=F32)


def _fdot(a, b):
    return jnp.dot(a, b, precision=HI, preferred_element_type=F32)


def _sigmoid(x):
    return 1.0 / (1.0 + jnp.exp(-x))


def _softplus(x):
    return jnp.maximum(x, 0.0) + jnp.log(1.0 + jnp.exp(-jnp.abs(x)))


def _silu(x):
    return x * _sigmoid(x)


def _cparams(n_axes):
    return pltpu.CompilerParams(dimension_semantics=("arbitrary",) * n_axes, vmem_limit_bytes=VMEM_LIMIT)


def _each(f, *lists):
    return [f(*xs) for xs in zip(*lists)]


def _unit_tri_inverse(ms, n):
    ri = lax.broadcasted_iota(jnp.int32, (n, n), 0)
    ci = lax.broadcasted_iota(jnp.int32, (n, n), 1)
    eye = (ri == ci).astype(F32)
    ts = _each(lambda m: eye - jnp.where((ri >> 1) == (ci >> 1), m, 0.0), ms)
    for lg in range(2, 7):
        couple = ((ri >> lg) == (ci >> lg)) & ((ri >> (lg - 1)) != (ci >> (lg - 1)))
        xs = _each(lambda m, t: _bdot(jnp.where(couple, m, 0.0), t), ms, ts)
        ts = _each(lambda t, x: t - _bdot(t, x), ts, xs)
    return ts


def _norm_proj_kernel(x_ref, g_ref, w_ref, o_ref, xn_ref):
    @pl.when(pl.program_id(1) == 0)
    def _():
        x = x_ref[...]
        y = x * lax.rsqrt(jnp.mean(x * x, axis=-1, keepdims=True) + NORM_EPS)
        xn_ref[...] = (y * g_ref[...]).astype(BF16)

    o_ref[...] = jnp.dot(xn_ref[...], w_ref[...], preferred_element_type=F32).astype(BF16)


def _norm_proj(x, g, w, tm, tn):
    n, d = x.shape
    cols = w.shape[1]
    return pl.pallas_call(
        _norm_proj_kernel,
        grid=(n // tm, cols // tn),
        in_specs=[pl.BlockSpec((tm, d), lambda i, j: (i, 0)),
                  pl.BlockSpec((1, d), lambda i, j: (0, 0)),
                  pl.BlockSpec((d, tn), lambda i, j: (0, j))],
        out_specs=pl.BlockSpec((tm, tn), lambda i, j: (i, j)),
        out_shape=jax.ShapeDtypeStruct((n, cols), BF16),
        scratch_shapes=[pltpu.VMEM((tm, d), BF16)],
        compiler_params=_cparams(2),
        name="norm_proj",
    )(x, g.reshape(1, d), w)


def _rw_prep_kernel(x_ref, xp_ref, xn_ref, mu_ref, w0_ref, w2_ref, a0_ref, a2_ref, g2_ref, kk_w_ref, ka_ref,
                    rk_ref, ones_ref, r_ref, v_ref, kk_ref, lw_ref, kd_ref, b_ref, bonus_ref, gate_ref,
                    *, tiles_per_seq):
    i = pl.program_id(0)
    tm = x_ref.shape[0]
    x = x_ref[...].astype(F32)
    first = (i % tiles_per_seq) == 0
    last = (i % tiles_per_seq) == tiles_per_seq - 1
    row = lax.broadcasted_iota(jnp.int32, x.shape, 0)
    prev_edge = jnp.where(first, 0.0, xp_ref[...].astype(F32)[HALO - 1:HALO, :])
    next_edge = jnp.where(last, 0.0, xn_ref[...].astype(F32)[0:1, :])
    prev = jnp.where(row == 0, prev_edge, pltpu.roll(x, 1, axis=0))
    nxt = jnp.where(row == tm - 1, next_edge, pltpu.roll(x, tm - 1, axis=0))
    f = x + (0.5 * (prev + nxt) - x) * mu_ref[...]
    w = RW_WIDTH
    r, k, v = f[:, 0:w], f[:, w:2 * w], f[:, 2 * w:3 * w]
    w_lo, a_lo, g_lo = f[:, 3 * w:3 * w + 128], f[:, 3 * w + 128:3 * w + 256], f[:, 3 * w + 256:3 * w + 384]
    ones_bd = ones_ref[...]
    kq = k * kk_w_ref[...]
    kk = kq * lax.rsqrt(_fdot(kq * kq, ones_bd) + NORM_EPS)
    gate_ref[...] = _bdot(_sigmoid(g_lo), g2_ref[...])
    w_log = -_softplus(-(w0_ref[...] + _bdot(jnp.tanh(w_lo), w2_ref[...]))) - 0.5
    lw_ref[...] = -jnp.exp(w_log)
    a_lr = _sigmoid(a0_ref[...] + _bdot(a_lo, a2_ref[...]))
    ka = ka_ref[...]
    kd0 = k * (1.0 + (a_lr[:, 0:w] - 1.0) * ka)
    kd1 = k * (1.0 + (a_lr[:, w:2 * w] - 1.0) * ka)
    kd_ref[:, 0:w] = kd0
    kd_ref[:, w:2 * w] = kd1
    b_ref[:, 0:w] = kk * a_lr[:, 0:w]
    b_ref[:, w:2 * w] = kk * a_lr[:, w:2 * w]
    bonus_ref[...] = _fdot(r * (kd0 + kd1) * rk_ref[...], ones_bd) * v
    r_ref[...] = r
    v_ref[...] = v
    kk_ref[...] = kk


def _rw_prep(proj, p, seqlen, tm):
    n = proj.shape[0]
    w = RW_WIDTH
    tps = seqlen // tm
    nhalo = n // HALO
    full = lambda a: pl.BlockSpec(a.shape, lambda i: (0,) * a.ndim)
    consts = [p['mu'], p['w0'], p['w2'], p['a0'], p['a2'], p['g2'], p['k_k'], p['k_a'], p['r_k'], p['ones']]
    out_w = [w, w, w, 2 * w, 2 * w, 2 * w, w, w]
    return pl.pallas_call(
        functools.partial(_rw_prep_kernel, tiles_per_seq=tps),
        grid=(n // tm,),
        in_specs=[pl.BlockSpec((tm, RW_PAD), lambda i: (i, COL_RW // RW_PAD)),
                  pl.BlockSpec((HALO, RW_PAD), lambda i: (jnp.maximum(i * (tm // HALO) - 1, 0), COL_RW // RW_PAD)),
                  pl.BlockSpec((HALO, RW_PAD), lambda i: (jnp.minimum((i + 1) * (tm // HALO), nhalo - 1),
                                                          COL_RW // RW_PAD))]
                 + [full(a) for a in consts],
        out_specs=[pl.BlockSpec((tm, c), lambda i: (i, 0)) for c in out_w],
        out_shape=[jax.ShapeDtypeStruct((n, c), F32) for c in out_w],
        compiler_params=_cparams(1),
        name="rw_prep",
    )(proj, proj, proj, *consts)


def _scan_masks(n, fwd):
    c = CHUNK
    ri = lax.broadcasted_iota(jnp.int32, (n, n), 0)
    ci = lax.broadcasted_iota(jnp.int32, (n, n), 1)
    ti, tj = ri & (c - 1), ci & (c - 1)
    same_head = (ri >> 6) == (ci >> 6)
    before = (tj < ti) if fwd else (tj > ti)
    strict = same_head & before
    incl = same_head & (before | (ti == tj))
    i64 = lax.broadcasted_iota(jnp.int32, (c, c), 0)
    j64 = lax.broadcasted_iota(jnp.int32, (c, c), 1)
    cum_mask = ((j64 <= i64) if fwd else (j64 >= i64)).astype(F32)
    return strict, incl, cum_mask


def _rw_chunks(fwd, masks, r, v, kk, lw, kd, bb, ht_ref, slots):
    c, pw = CHUNK, 128
    strict, incl, cum_mask = masks
    head0 = lax.broadcasted_iota(jnp.int32, (c, pw), 1) < RW_HEAD_DIM
    stack = lambda x: jnp.concatenate([jnp.where(head0, x, 0.0), jnp.where(head0, 0.0, x)], axis=0)
    cat0 = lambda a, b: jnp.concatenate([a, b], axis=0)

    cum = _each(lambda x: _fdot(cum_mask, x), lw)
    tot = _each(lambda x: x[c - 1:c, :] if fwd else x[0:1, :], cum)
    p_inv = _each(lambda x: jnp.exp(-x), cum)
    p_end_over = _each(lambda t, x: jnp.exp(t - x), tot, cum)
    rd_s = _each(lambda x, cm: stack(x * jnp.exp(cm)), r, cum)
    kp_s = _each(lambda x, cm, l: stack(x * jnp.exp(cm - l)), kk, cum, lw)
    kb_inv = _each(lambda k, b, pi: cat0(stack(k * pi), stack(b * pi)), kd, bb, p_inv)
    kb_end = _each(lambda k, b, pe: cat0(stack(k * pe), stack(b * pe)), kd, bb, p_end_over)
    v_s = _each(stack, v)
    a_k = _each(_bdot_nt, kp_s, kb_inv)
    a_r = _each(_bdot_nt, rd_s, kb_inv)
    t_inv = _unit_tri_inverse(_each(lambda a: jnp.where(strict, a[:, pw:], 0.0), a_k), pw)
    ht = [ht_ref[s] for s in slots]
    carry = _each(lambda kp, rd, h: _bdot_nt(cat0(kp, rd), h), kp_s, rd_s, ht)
    av = _each(lambda a, x: _bdot(jnp.where(strict, a[:, :pw], 0.0), x), a_k, v_s)
    u_s = _each(lambda t, cr, x: _bdot(t, cr[:pw] + x), t_inv, carry, av)
    y_s = _each(lambda a, cr, x, u: cr[pw:] + _bdot(
        jnp.concatenate([jnp.where(incl, a[:, :pw], 0.0), -jnp.where(incl, a[:, pw:], 0.0)], axis=1), cat0(x, u)),
        a_r, carry, v_s, u_s)
    upd = _each(lambda x, u, ke: _bdot_tn(cat0(x, -u), ke), v_s, u_s, kb_end)
    for s, h, t, up in zip(slots, ht, tot, upd):
        ht_ref[s] = h * jnp.exp(t) + up
    return _each(lambda y: y[:c] + y[c:], y_s)


def _rw_scan_kernel(*refs, sb):
    ins, (yf_ref, yb_ref, ht_ref) = refs[:12], refs[12:]

    @pl.when(pl.program_id(1) == 0)
    def _():
        ht_ref[...] = jnp.zeros_like(ht_ref)

    npair = RW_WIDTH // 128
    for d, y_ref in enumerate((yf_ref, yb_ref)):
        items = [(s, p) for s in range(sb) for p in range(npair)]
        tiles = [[ref[s, :, 128 * p:128 * (p + 1)] for s, p in items] for ref in ins[6 * d:6 * d + 6]]
        ys = _rw_chunks(d == 0, _scan_masks(128, d == 0), *tiles, ht_ref,
                        [(d * sb + s) * npair + p for s, p in items])
        for (s, p), y in zip(items, ys):
            y_ref[s, :, 128 * p:128 * (p + 1)] = y


def _scan_batch(bsz):
    return next(s for s in (4, 2, 1) if bsz % s == 0)


def _rw_scan(r, v, kk, lw, kd, b, bsz, seqlen):
    n, w = r.shape
    nc = seqlen // CHUNK
    sb = _scan_batch(bsz)
    r, v, kk = (a.reshape(bsz, seqlen, w) for a in (r, v, kk))
    lw, kd, b = (a.reshape(bsz, seqlen, 2 * w) for a in (lw, kd, b))
    spec = lambda d, col: pl.BlockSpec((sb, CHUNK, w), lambda bi, c: (bi, c + d * (nc - 1 - 2 * c), col))
    yf, yb = pl.pallas_call(
        functools.partial(_rw_scan_kernel, sb=sb),
        grid=(bsz // sb, nc),
        in_specs=[spec(0, 0)] * 6 + [spec(1, 0)] * 3 + [spec(1, 1)] * 3,
        out_specs=[spec(0, 0), spec(1, 0)],
        out_shape=[jax.ShapeDtypeStruct((bsz, seqlen, w), F32)] * 2,
        scratch_shapes=[pltpu.VMEM((2 * sb * (w // 128), 128, 128), F32)],
        compiler_params=_cparams(2),
        name="rw_scan",
    )(r, v, kk, lw, kd, b, r, v, kk, lw, kd, b)
    return yf.reshape(n, w), yb.reshape(n, w)


def _rw_params(p, layer):
    w = RW_WIDTH
    pad = lambda a: jnp.pad(a, ((0, 0), (0, RW_PAD - a.shape[1])))
    blockdiag2 = lambda m: jnp.concatenate(
        [jnp.concatenate([m[0], jnp.zeros_like(m[0])], axis=1),
         jnp.concatenate([jnp.zeros_like(m[1]), m[1]], axis=1)], axis=0)
    hid = jnp.arange(w) // RW_HEAD_DIM
    return {
        'mu': pad(p['rw_mu'][layer][None, :]),
        'w0': p['rw_w0'][layer].reshape(1, 2 * w),
        'w2': blockdiag2(p['rw_w2'][layer]).astype(BF16),
        'a0': p['rw_a0'][layer].reshape(1, 2 * w),
        'a2': blockdiag2(p['rw_a2'][layer]).astype(BF16),
        'g2': p['rw_g2'][layer].astype(BF16),
        'k_k': p['rw_k_k'][layer][None, :],
        'k_a': p['rw_k_a'][layer][None, :],
        'r_k': p['rw_r_k'][layer].reshape(1, w),
        'ones': (hid[:, None] == hid[None, :]).astype(F32),
    }


def _dn_prep_kernel(x_ref, xp_ref, xn_ref, ba_ref, cw_ref, alog_ref, dtb_ref, ones_ref,
                    q_ref, k_ref, v_ref, bg_ref, *, tiles_per_seq):
    i = pl.program_id(0)
    tm = x_ref.shape[0]
    x = x_ref[...].astype(F32)
    first = (i % tiles_per_seq) == 0
    last = (i % tiles_per_seq) == tiles_per_seq - 1
    row = lax.broadcasted_iota(jnp.int32, x.shape, 0)
    xp = jnp.where(first, 0.0, xp_ref[...].astype(F32))
    xn = jnp.where(last, 0.0, xn_ref[...].astype(F32))
    pad = (DN_CONV - 1) // 2
    acc = x * cw_ref[pad:pad + 1, :]
    for s in range(1, pad + 1):
        back = pltpu.roll(x, s, axis=0)
        fore = pltpu.roll(x, tm - s, axis=0)
        for t in range(s):
            back = jnp.where(row == t, xp[HALO - s + t:HALO + 1 - s + t, :], back)
            fore = jnp.where(row == tm - 1 - t, xn[s - 1 - t:s - t, :], fore)
        acc = acc + back * cw_ref[pad - s:pad - s + 1, :] + fore * cw_ref[pad + s:pad + s + 1, :]
    y = _silu(acc)
    w = DN_WIDTH
    ones_bd = ones_ref[...]
    q, k = y[:, 0:w], y[:, w:2 * w]
    q_ref[...] = q * (lax.rsqrt(_fdot(q * q, ones_bd) + NORM_EPS) * (DN_HEAD_DIM ** -0.5))
    k_ref[...] = k * lax.rsqrt(_fdot(k * k, ones_bd) + NORM_EPS)
    v_ref[...] = y[:, 2 * w:3 * w]
    ba = ba_ref[...].astype(F32)
    lane = lax.broadcasted_iota(jnp.int32, ba.shape, 1)
    g = -jnp.exp(alog_ref[...]) * _softplus(ba + dtb_ref[...])
    bg_ref[...] = jnp.where((lane & 4) == 0, _sigmoid(ba), g)


def _dn_prep(proj, p, seqlen, tm):
    n = proj.shape[0]
    w = DN_WIDTH
    tps = seqlen // tm
    nhalo = n // HALO
    full = lambda a: pl.BlockSpec(a.shape, lambda i: (0,) * a.ndim)
    consts = [p['conv_w'], p['a_log'], p['dt_bias'], p['ones']]
    cb = COL_QKV // (3 * w)
    return pl.pallas_call(
        functools.partial(_dn_prep_kernel, tiles_per_seq=tps),
        grid=(n // tm,),
        in_specs=[pl.BlockSpec((tm, 3 * w), lambda i: (i, cb)),
                  pl.BlockSpec((HALO, 3 * w), lambda i: (jnp.maximum(i * (tm // HALO) - 1, 0), cb)),
                  pl.BlockSpec((HALO, 3 * w), lambda i: (jnp.minimum((i + 1) * (tm // HALO), nhalo - 1), cb)),
                  pl.BlockSpec((tm, 128), lambda i: (i, COL_BA // 128))]
                 + [full(a) for a in consts],
        out_specs=[pl.BlockSpec((tm, c), lambda i: (i, 0)) for c in (w, w, w, 128)],
        out_shape=[jax.ShapeDtypeStruct((n, c), F32) for c in (w, w, w, 128)],
        compiler_params=_cparams(1),
        name="dn_prep",
    )(proj, proj, proj, proj, *consts)


def _dn_chunks(fwd, masks, q, k, v, bg, s_ref, slots):
    c, hd, nh = CHUNK, DN_HEAD_DIM, DN_HEADS
    n = nh * c
    heads = range(nh)
    strict, incl = [m[0] for m in masks], [m[1] for m in masks]
    lane0 = [0 if f else 8 for f in fwd]
    gcum_all = _each(lambda m, x: _fdot(m[2], x), masks, bg)
    bgd = _each(lambda x, l: x[:, l:l + 8], bg, lane0)
    gcum = _each(lambda x, l: x[:, l:l + 8], gcum_all, lane0)
    gtot = _each(lambda x, f: x[c - 1:c, :] if f else x[0:1, :], gcum, fwd)
    beta_s = _each(lambda x: jnp.concatenate([x[:, h:h + 1] for h in heads], axis=0), bgd)
    gc_s = _each(lambda x: jnp.concatenate([x[:, nh + h:nh + h + 1] for h in heads], axis=0), gcum)
    gend_s = _each(lambda x: jnp.concatenate(
        [jnp.broadcast_to(x[:, nh + h:nh + h + 1], (c, 1)) for h in heads], axis=0), gtot)
    stack = lambda x: jnp.concatenate([x[:, hd * h:hd * (h + 1)] for h in heads], axis=0)
    q_s, k_s, v_s = _each(stack, q), _each(stack, k), _each(stack, v)

    def decay_of(g, inc):
        rows = jnp.broadcast_to(g, (n, n))
        return jnp.exp(jnp.where(inc, rows - rows.T, -1e30))

    decay = _each(decay_of, gc_s, incl)
    kb_s = _each(lambda x, b: x * b, k_s, beta_s)
    qk = _each(lambda kb, x, kk: _bdot_nt(jnp.concatenate([kb, x], axis=0), kk), kb_s, q_s, k_s)
    t_inv = _unit_tri_inverse(_each(lambda a, dc, st: jnp.where(st, a[:n] * dc, 0.0), qk, decay, strict), n)
    a_qk = _each(lambda a, dc, inc: jnp.where(inc, a[n:] * dc, 0.0), qk, decay, incl)
    e_gc = _each(jnp.exp, gc_s)
    uw = _each(lambda t, x, b, kb, e: _bdot(t, jnp.concatenate([x * b, kb * e], axis=1)),
               t_inv, v_s, beta_s, kb_s, e_gc)
    q_dec = _each(lambda x, e: x * e, q_s, e_gc)
    k_dec = _each(lambda x, ge, g: x * jnp.exp(ge - g), k_s, gend_s, gc_s)
    e_end = _each(jnp.exp, gtot)
    rows = [slice(c * h, c * (h + 1)) for h in heads]
    states = [[s_ref[sl + h] for h in heads] for sl in slots]
    carry = _each(lambda x, qd, st: [_bdot(jnp.concatenate([x[rows[h], hd:], qd[rows[h]]], axis=0), st[h])
                                     for h in heads], uw, q_dec, states)
    v_new = _each(lambda x, cr: x[:, :hd] - jnp.concatenate([cr[h][:c] for h in heads], axis=0), uw, carry)
    o_s = _each(lambda cr, a, x: jnp.concatenate([cr[h][c:] for h in heads], axis=0) + _bdot(a, x),
                carry, a_qk, v_new)
    upd = _each(lambda kd, x: [_bdot_tn(kd[rows[h]], x[rows[h]]) for h in heads], k_dec, v_new)
    for sl, st, e, up in zip(slots, states, e_end, upd):
        for h in heads:
            s_ref[sl + h] = st[h] * e[:, nh + h:nh + h + 1] + up[h]
    return _each(lambda x: jnp.concatenate([x[rows[h]] for h in heads], axis=1), o_s)


def _dn_scan_kernel(*refs, sb):
    ins, outs, s_ref = refs[:8], refs[8:10], refs[10]

    @pl.when(pl.program_id(1) == 0)
    def _():
        s_ref[...] = jnp.zeros_like(s_ref)

    items = [(d, s) for d in range(2) for s in range(sb)]
    both = [_scan_masks(DN_HEADS * CHUNK, True), _scan_masks(DN_HEADS * CHUNK, False)]
    tiles = [[ins[4 * d + j][s] for d, s in items] for j in range(4)]
    os_ = _dn_chunks([d == 0 for d, s in items], [both[d] for d, s in items], *tiles, s_ref,
                     [(d * sb + s) * DN_HEADS for d, s in items])
    for (d, s), o in zip(items, os_):
        outs[d][s] = o


def _dn_scan(q, k, v, bg, bsz, seqlen):
    n, w = q.shape
    nc = seqlen // CHUNK
    sb = _scan_batch(bsz)
    q, k, v = (a.reshape(bsz, seqlen, w) for a in (q, k, v))
    bg = bg.reshape(bsz, seqlen, 128)
    spec = lambda d, cols: pl.BlockSpec((sb, CHUNK, cols), lambda bi, c: (bi, c + d * (nc - 1 - 2 * c), 0))
    dir_specs = lambda d: [spec(d, w)] * 3 + [spec(d, 128)]
    of, ob = pl.pallas_call(
        functools.partial(_dn_scan_kernel, sb=sb),
        grid=(bsz // sb, nc),
        in_specs=dir_specs(0) + dir_specs(1),
        out_specs=[spec(0, w), spec(1, w)],
        out_shape=[jax.ShapeDtypeStruct((bsz, seqlen, w), F32)] * 2,
        scratch_shapes=[pltpu.VMEM((2 * sb * DN_HEADS, DN_HEAD_DIM, DN_HEAD_DIM), F32)],
        compiler_params=_cparams(2),
        name="dn_scan",
    )(q, k, v, bg, q, k, v, bg)
    return of.reshape(n, w), ob.reshape(n, w)


def _dn_params(p, layer):
    lane = jnp.arange(128)
    dirn, hh, is_alpha = (lane >> 3) & 1, lane & 3, ((lane & 4) != 0) & (lane < 16)
    a_log = jnp.where(is_alpha, p['dn_a_log'][layer][dirn, hh], 0.0)
    dt_bias = jnp.where(is_alpha, p['dn_dt_bias'][layer][dirn, hh], 0.0)
    hid = jnp.arange(DN_WIDTH) // DN_HEAD_DIM
    return {
        'conv_w': jnp.pad(p['dn_conv_w'][layer], ((0, 8 - DN_CONV), (0, 0))),
        'a_log': a_log[None, :].astype(F32),
        'dt_bias': dt_bias[None, :].astype(F32),
        'ones': (hid[:, None] == hid[None, :]).astype(F32),
    }


S5_CHUNK = 16
S5_FLAT = S5_CHUNK * S5_GROUP


def _s5_kernel(u_ref, ws_ref, wy_ref, ca_ref, cb_ref, y_ref, *, levels):
    u = u_ref[...]
    nc = u.shape[0]
    x = _bdot(u, ws_ref[...])
    row = lax.broadcasted_iota(jnp.int32, (nc, 128), 0)
    ca, cb = ca_ref[...], cb_ref[...]

    def cmul(k, half, h):
        a = ca[k:k + 1, 128 * half:128 * (half + 1)]
        b = cb[k:k + 1, 128 * half:128 * (half + 1)]
        return a * h + b * pltpu.roll(h, S5_STATE, axis=1)

    hf, hb = x[:, 0:128], x[:, 128:256]
    for k in range(levels):
        s = 1 << k
        hf = hf + jnp.where(row >= s, cmul(k, 0, pltpu.roll(hf, s, axis=0)), 0.0)
        hb = hb + jnp.where(row < nc - s, cmul(k, 1, pltpu.roll(hb, nc - s, axis=0)), 0.0)
    hf = jnp.where(row >= 1, pltpu.roll(hf, 1, axis=0), 0.0)
    hb = jnp.where(row < nc - 1, pltpu.roll(hb, nc - 1, axis=0), 0.0)
    y_ref[...] = _bdot(jnp.concatenate([u.astype(BF16), hf.astype(BF16), hb.astype(BF16)], axis=1), wy_ref[...])


def _s5_core(proj, sp, bsz, seqlen):
    n = proj.shape[0]
    nc = seqlen // S5_CHUNK
    u = lax.slice_in_dim(proj, COL_S5, COL_S5 + S5_WIDTH, axis=1)
    ug = u.reshape(bsz, nc, S5_CHUNK, S5_GROUPS, S5_GROUP).transpose(3, 0, 1, 2, 4).reshape(S5_GROUPS, bsz * nc, S5_FLAT)
    levels = sp['ca'].shape[1]
    wspec = lambda a: pl.BlockSpec((None,) + a.shape[1:], lambda g, b: (g, 0, 0))
    yg = pl.pallas_call(
        functools.partial(_s5_kernel, levels=levels),
        grid=(S5_GROUPS, bsz),
        in_specs=[pl.BlockSpec((None, nc, S5_FLAT), lambda g, b: (g, b, 0)),
                  wspec(sp['ws']), wspec(sp['wy']), wspec(sp['ca']), wspec(sp['cb'])],
        out_specs=pl.BlockSpec((None, nc, S5_FLAT), lambda g, b: (g, b, 0)),
        out_shape=jax.ShapeDtypeStruct((S5_GROUPS, bsz * nc, S5_FLAT), F32),
        compiler_params=_cparams(2),
        name="s5_core",
    )(ug, sp['ws'], sp['wy'], sp['ca'], sp['cb'])
    return yg.reshape(S5_GROUPS, bsz, nc, S5_CHUNK, S5_GROUP).transpose(1, 2, 3, 0, 4).reshape(n, S5_WIDTH)


def _s5_params(p, layer, seqlen):
    cs, gs, ps = S5_CHUNK, S5_GROUP, S5_STATE
    cexp = lambda zr, zi: (jnp.exp(zr) * jnp.cos(zi), jnp.exp(zr) * jnp.sin(zi))
    cmul = lambda a, b: (a[0] * b[0] - a[1] * b[1], a[0] * b[1] + a[1] * b[0])
    lam_re, lam_im = p['s5_lam_re'][layer].astype(F32), p['s5_lam_im'][layer].astype(F32)
    step = jnp.exp(p['s5_log_step'][layer].astype(F32))[..., None]
    zr, zi = lam_re * step, lam_im * step
    lbar_re, lbar_im = cexp(zr, zi)
    den = lam_re * lam_re + lam_im * lam_im
    num_re = lbar_re - 1.0
    f = ((num_re * lam_re + lbar_im * lam_im) / den, (lbar_im * lam_re - num_re * lam_im) / den)
    bbar = cmul((f[0][..., None], f[1][..., None]),
                (p['s5_b_re'][layer].astype(F32), p['s5_b_im'][layer].astype(F32)))
    cc = (p['s5_c_re'][layer].astype(F32), p['s5_c_im'][layer].astype(F32))
    tau = jnp.arange(cs + 1, dtype=F32)
    pw = cexp(zr[..., None] * tau, zi[..., None] * tau)
    m = cmul((pw[0][..., None], pw[1][..., None]), (bbar[0][..., None, :], bbar[1][..., None, :]))
    kern = (jnp.einsum('dgcp,dgpte->dgtce', cc[0], m[0], precision=HI)
            - jnp.einsum('dgcp,dgpte->dgtce', cc[1], m[1], precision=HI))
    s_i, t_i = jnp.arange(cs)[:, None], jnp.arange(cs)[None, :]
    kf = jnp.where((t_i >= s_i)[None, :, :, None, None], kern[0][:, jnp.clip(t_i - s_i, 0, cs)], 0.0)
    kb = jnp.where((s_i >= t_i)[None, :, :, None, None], kern[1][:, jnp.clip(s_i - t_i, 0, cs)], 0.0)
    tmat = (kf + kb).transpose(0, 1, 4, 2, 3).reshape(S5_GROUPS, cs * gs, cs * gs)
    sidx = jnp.arange(cs)
    sel = lambda d, idx: (pw[0][d][..., idx][..., None], pw[1][d][..., idx][..., None])
    inj = lambda d: (bbar[0][d][:, :, None, :], bbar[1][d][:, :, None, :])
    ws_f = cmul(sel(0, cs - 1 - sidx), inj(0))
    ws_b = cmul(sel(1, sidx), inj(1))
    to_cols = lambda w: jnp.concatenate([w[0], w[1]], axis=1).transpose(0, 2, 3, 1).reshape(
        S5_GROUPS, cs * gs, 2 * ps)
    ws = jnp.concatenate([to_cols(ws_f), to_cols(ws_b)], axis=2)
    out = lambda d: (cc[0][d].transpose(0, 2, 1)[:, :, None, :], cc[1][d].transpose(0, 2, 1)[:, :, None, :])
    wo_f = cmul(out(0), sel(0, 1 + sidx))
    wo_b = cmul(out(1), sel(1, cs - sidx))
    to_rows = lambda w: jnp.concatenate([w[0], -w[1]], axis=1).reshape(S5_GROUPS, 2 * ps, cs * gs)
    wy = jnp.concatenate([tmat, to_rows(wo_f), to_rows(wo_b)], axis=1)
    levels = max(1, (seqlen // cs - 1).bit_length())
    span = cs * (2.0 ** jnp.arange(levels, dtype=F32))
    lk = cexp(zr[..., None] * span, zi[..., None] * span)
    re, im = lk[0].transpose(0, 1, 3, 2), lk[1].transpose(0, 1, 3, 2)
    ca = jnp.concatenate([re[0], re[0], re[1], re[1]], axis=-1)
    cb = jnp.concatenate([-im[0], im[0], -im[1], im[1]], axis=-1)
    return {'ws': ws.astype(BF16), 'wy': wy.astype(BF16), 'ca': ca.astype(F32), 'cb': cb.astype(F32)}


def _group_sum(x, ones_bf16):
    hi = x.astype(BF16)
    lo = (x - hi.astype(F32)).astype(BF16)
    return (jnp.dot(hi, ones_bf16, preferred_element_type=F32) + jnp.dot(lo, ones_bf16, preferred_element_type=F32))


def _gelu_tanh(x):
    return 0.5 * x * (1.0 + jnp.tanh(math.sqrt(2.0 / math.pi) * (x + 0.044715 * (x * x * x))))


def _merge_kernel(h_ref, gate_ref, z_ref, u_ref, odnf_ref, odnb_ref, ys5_ref, yrwf_ref, yrwb_ref, bonus_ref,
                  grw_ref, ones128_ref, ones64_ref, dng_ref, s5d_ref, glub_ref, lng_ref, lnb_ref,
                  gluw_ref, dnp_ref, s5p_ref, rwp_ref, wout_ref, o_ref):
    o = odnf_ref[...] + odnb_ref[...]
    ms = _group_sum(o * o, ones128_ref[...]) * (1.0 / DN_HEAD_DIM)
    o_a = o * lax.rsqrt(ms + NORM_EPS) * dng_ref[...] * _silu(z_ref[...].astype(F32))
    y = _gelu_tanh(u_ref[...].astype(F32) * s5d_ref[...] + ys5_ref[...])
    o_b = y * _sigmoid(_bdot(y, gluw_ref[...]) + glub_ref[...])
    yr = yrwf_ref[...] + yrwb_ref[...]
    ones64 = ones64_ref[...]
    cen = yr - _group_sum(yr, ones64) * (1.0 / RW_HEAD_DIM)
    var = _group_sum(cen * cen, ones64) * (1.0 / RW_HEAD_DIM)
    o_c = (cen * lax.rsqrt(var + RW_GN_EPS) * lng_ref[...] + lnb_ref[...] + bonus_ref[...]) * grw_ref[...]
    dm = D_MODEL
    gate = lambda j: _sigmoid(gate_ref[:, j * dm:(j + 1) * dm].astype(F32))
    merged = (gate(0) * _bdot(o_a, dnp_ref[...]) + gate(1) * _bdot(o_b, s5p_ref[...])
              + gate(2) * _bdot(o_c, rwp_ref[...]))
    o_ref[...] = h_ref[...] + _bdot(merged, wout_ref[...])


def _merge(h, proj, o_dn, y_s5, y_rw, bonus, gate_rw, mp, tm):
    n, dm = h.shape
    w = DN_WIDTH
    row = lambda cols, cb=0: pl.BlockSpec((tm, cols), lambda i: (i, cb))
    full = lambda a: pl.BlockSpec(a.shape, lambda i: (0,) * a.ndim)
    consts = [mp['ones128'], mp['ones64'], mp['dn_norm_g'], mp['s5_d'], mp['glu_b'], mp['ln_g'], mp['ln_b'],
              mp['glu_w'], mp['dn_proj'], mp['s5_proj'], mp['rw_proj'], mp['w_out']]
    return pl.pallas_call(
        _merge_kernel,
        grid=(n // tm,),
        in_specs=[row(dm), row(3 * dm, COL_GATE // (3 * dm)), row(w, COL_Z // w), row(w, COL_S5 // w),
                  row(w), row(w), row(w), row(w), row(w), row(w), row(w)] + [full(a) for a in consts],
        out_specs=row(dm),
        out_shape=jax.ShapeDtypeStruct((n, dm), F32),
        compiler_params=_cparams(1),
        name="merge",
    )(h, proj, proj, proj, o_dn[0], o_dn[1], y_s5, y_rw[0], y_rw[1], bonus, gate_rw, *consts)


def _merge_params(p, layer):
    hid128 = jnp.arange(DN_WIDTH) // DN_HEAD_DIM
    hid64 = jnp.arange(RW_WIDTH) // RW_HEAD_DIM
    r1 = lambda a: a.reshape(1, -1).astype(F32)
    return {
        'ones128': (hid128[:, None] == hid128[None, :]).astype(BF16),
        'ones64': (hid64[:, None] == hid64[None, :]).astype(BF16),
        'dn_norm_g': r1(jnp.tile(p['dn_norm_g'][layer], DN_HEADS)),
        's5_d': r1(p['s5_d'][layer]), 'glu_b': r1(p['s5_glu_b'][layer]),
        'ln_g': r1(p['rw_ln_g'][layer]), 'ln_b': r1(p['rw_ln_b'][layer]),
        'glu_w': p['s5_glu_w'][layer].astype(BF16), 'dn_proj': p['dn_proj'][layer].astype(BF16),
        's5_proj': p['s5_proj'][layer].astype(BF16), 'rw_proj': p['rw_proj'][layer].astype(BF16),
        'w_out': p['w_out'][layer].astype(BF16),
    }


MOE_TILE = 128
MOE_WIN = MOE_TILE + 16
MOE_GATHER_TILES = 16
MOE_ROWS = 512


def _slot_rows(cap):
    tr = min(MOE_ROWS, cap)
    return -(-(cap + MOE_WIN) // tr) * tr


def _router_kernel(h_ref, g_ref, rw_ref, hn_ref, aff_ref):
    x = h_ref[...]
    hn = (x * lax.rsqrt(jnp.mean(x * x, axis=-1, keepdims=True) + NORM_EPS) * g_ref[...]).astype(BF16)
    hn_ref[...] = hn
    logits = lax.dot_general(rw_ref[...], hn, (((1,), (1,)), ((), ())), preferred_element_type=F32)
    ex = jnp.exp(logits - jnp.max(logits, axis=0, keepdims=True))
    aff_ref[...] = ex / jnp.sum(ex, axis=0, keepdims=True)


def _router(h, g, router_wt, tm):
    n, dm = h.shape
    return pl.pallas_call(
        _router_kernel,
        grid=(n // tm,),
        in_specs=[pl.BlockSpec((tm, dm), lambda i: (i, 0)), pl.BlockSpec((1, dm), lambda i: (0, 0)),
                  pl.BlockSpec((N_EXPERTS, dm), lambda i: (0, 0))],
        out_specs=[pl.BlockSpec((tm, dm), lambda i: (i, 0)), pl.BlockSpec((N_EXPERTS, tm), lambda i: (0, i))],
        out_shape=[jax.ShapeDtypeStruct((n, dm), BF16), jax.ShapeDtypeStruct((N_EXPERTS, n), F32)],
        compiler_params=_cparams(1),
        name="router",
    )(h, g.reshape(1, dm), router_wt)


def _threshold_kernel(aff_ref, thr_ref, *, cap):
    bits = pltpu.bitcast(aff_ref[...], jnp.int32)

    def body(i, t):
        cand = t | (jnp.int32(1) << (30 - i))
        cnt = jnp.sum((bits >= cand).astype(F32), axis=1, keepdims=True)
        return jnp.where(cnt >= cap, cand, t)

    t = lax.fori_loop(0, 31, body, jnp.zeros((N_EXPERTS, 1), jnp.int32))
    thr_ref[...] = jnp.broadcast_to(t, thr_ref.shape)


def _threshold(aff_t, cap):
    e, n = aff_t.shape
    return pl.pallas_call(
        functools.partial(_threshold_kernel, cap=cap),
        grid=(1,),
        in_specs=[pl.BlockSpec((e, n), lambda i: (0, 0))],
        out_specs=pl.BlockSpec((e, 128), lambda i: (0, 0)),
        out_shape=jax.ShapeDtypeStruct((e, 128), jnp.int32),
        compiler_params=_cparams(1),
        name="moe_threshold",
    )(aff_t)


def _slots_kernel(thr_ref, aff_ref, slot_ref, posx_ref, *, cap):
    e = pl.program_id(0)
    thr = thr_ref[e]
    bits = pltpu.bitcast(aff_ref[...], jnp.int32)
    nb = bits.shape[0]
    li = lax.broadcasted_iota(jnp.int32, (128, 128), 0)
    lj = lax.broadcasted_iota(jnp.int32, (128, 128), 1)
    upper = (li <= lj).astype(BF16)
    bi = lax.broadcasted_iota(jnp.int32, (nb, nb), 0)
    bj = lax.broadcasted_iota(jnp.int32, (nb, nb), 1)
    lower_strict = (bj < bi).astype(BF16)

    def exclusive_count(m):
        incl = jnp.dot(m.astype(BF16), upper, preferred_element_type=F32)
        tot = jnp.broadcast_to(incl[:, 127:128], (nb, 128))
        offs = jnp.dot(lower_strict, tot.astype(BF16), preferred_element_type=F32)
        return offs + incl - m

    gt = (bits > thr).astype(F32)
    eq = (bits == thr).astype(F32)
    need = cap - jnp.sum(jnp.sum(gt, axis=1, keepdims=True), axis=0, keepdims=True)
    sel = jnp.maximum(gt, eq * (exclusive_count(eq) < need).astype(F32))
    pos = exclusive_count(sel).astype(jnp.int32)
    posx_ref[...] = pos
    slot_ref[...] = jnp.where(sel > 0.0, pos, -1)


def _slots(aff_t, thr, cap):
    e, n = aff_t.shape
    nb = n // 128
    blk = pl.BlockSpec((nb, 128), lambda i, thr_ref: (i, 0))
    slot, posx = pl.pallas_call(
        functools.partial(_slots_kernel, cap=cap),
        grid_spec=pltpu.PrefetchScalarGridSpec(
            num_scalar_prefetch=1, grid=(e,), in_specs=[blk], out_specs=[blk, blk]),
        out_shape=[jax.ShapeDtypeStruct((e * nb, 128), jnp.int32)] * 2,
        compiler_params=_cparams(1),
        name="moe_slots",
    )(thr[:, 0], aff_t.reshape(e * nb, 128))
    return slot.reshape(e, n), posx.reshape(e, n)


def _gather_kernel(base_ref, slot_ref, hn_ref, xs_ref, *, nsub):
    e, k = pl.program_id(0), pl.program_id(1)

    @pl.when(k == 0)
    def _():
        xs_ref[...] = jnp.zeros_like(xs_ref)

    tile0 = (e * pl.num_programs(1) + k) * nsub
    srow = lax.broadcasted_iota(jnp.int32, (MOE_WIN, MOE_TILE), 0)

    def body(j, carry):
        start = pl.multiple_of((base_ref[tile0 + j] >> 4) << 4, 16)
        onehot = (slot_ref[pl.ds(j, 1), :] - start == srow).astype(BF16)
        tok = hn_ref[pl.ds(pl.multiple_of(j * MOE_TILE, MOE_TILE), MOE_TILE), :]
        rows = jnp.dot(onehot, tok, preferred_element_type=F32).astype(BF16)
        win = pl.ds(start, MOE_WIN)
        xs_ref[win, :] = xs_ref[win, :] + rows
        return carry

    lax.fori_loop(0, nsub, body, 0)


def _gather(hn, slot, base, cap):
    n, dm = hn.shape
    nsub = min(MOE_GATHER_TILES, n // MOE_TILE)
    ngs = n // (nsub * MOE_TILE)
    rows = _slot_rows(cap)
    return pl.pallas_call(
        functools.partial(_gather_kernel, nsub=nsub),
        grid_spec=pltpu.PrefetchScalarGridSpec(
            num_scalar_prefetch=1, grid=(N_EXPERTS, ngs),
            in_specs=[pl.BlockSpec((None, nsub, MOE_TILE), lambda e, k, b: (e * ngs + k, 0, 0)),
                      pl.BlockSpec((nsub * MOE_TILE, dm), lambda e, k, b: (k, 0))],
            out_specs=pl.BlockSpec((None, rows, dm), lambda e, k, b: (e, 0, 0))),
        out_shape=jax.ShapeDtypeStruct((N_EXPERTS, rows, dm), BF16),
        compiler_params=_cparams(2),
        name="moe_gather",
    )(base, slot.reshape(N_EXPERTS * ngs, nsub, MOE_TILE), hn)


def _ffn_kernel(x_ref, w1_ref, w3_ref, w2_ref, o_ref):
    x = x_ref[...]
    a = jnp.dot(x, w1_ref[...], preferred_element_type=F32)
    g = jnp.dot(x, w3_ref[...], preferred_element_type=F32)
    o_ref[...] = jnp.dot((_silu(a) * g).astype(BF16), w2_ref[...], preferred_element_type=F32).astype(BF16)


def _ffn(xs, w1, w3, w2, cap):
    e, rows, dm = xs.shape
    ff = w1.shape[2]
    tr = min(MOE_ROWS, cap)
    return pl.pallas_call(
        _ffn_kernel,
        grid=(e, rows // tr),
        in_specs=[pl.BlockSpec((None, tr, dm), lambda i, r: (i, r, 0)),
                  pl.BlockSpec((None, dm, ff), lambda i, r: (i, 0, 0)),
                  pl.BlockSpec((None, dm, ff), lambda i, r: (i, 0, 0)),
                  pl.BlockSpec((None, ff, dm), lambda i, r: (i, 0, 0))],
        out_specs=pl.BlockSpec((None, tr, dm), lambda i, r: (i, r, 0)),
        out_shape=jax.ShapeDtypeStruct((e, rows, dm), BF16),
        compiler_params=_cparams(2),
        name="moe_ffn",
    )(xs, w1, w3, w2)


def _combine_kernel(base_ref, h_ref, slot_ref, gate_ref, g_ref, *rest, final_norm):
    wins, o_ref = rest[:N_EXPERTS], rest[N_EXPERTS]
    k = pl.program_id(0)
    nt = pl.num_programs(0)
    scol = lax.broadcasted_iota(jnp.int32, (MOE_TILE, MOE_WIN), 1)
    slot, gate = slot_ref[...], gate_ref[...]
    acc = h_ref[...]
    for e in range(N_EXPERTS):
        start = (base_ref[e * nt + k] >> 4) << 4
        onehot = (slot[:, e:e + 1] - start == scol).astype(BF16)
        acc = acc + jnp.dot(onehot, wins[e][...], preferred_element_type=F32) * gate[:, e:e + 1]
    if final_norm:
        acc = acc * lax.rsqrt(jnp.mean(acc * acc, axis=-1, keepdims=True) + NORM_EPS) * g_ref[...]
    o_ref[...] = acc


def _combine(h, slot_tm, gate_tm, base, outs, final_g, cap):
    n, dm = h.shape
    nt = n // MOE_TILE
    rows = outs.shape[1]
    outs = outs.reshape(N_EXPERTS * rows, dm)
    tile = lambda cols: pl.BlockSpec((MOE_TILE, cols), lambda k, b: (k, 0))

    def window(e):
        return pl.BlockSpec((pl.Element(MOE_WIN), pl.Element(dm)),
                            lambda k, b: (pl.multiple_of(e * rows + ((b[e * nt + k] >> 4) << 4), 16), 0))

    g = jnp.ones((1, dm), F32) if final_g is None else final_g.reshape(1, dm).astype(F32)
    return pl.pallas_call(
        functools.partial(_combine_kernel, final_norm=final_g is not None),
        grid_spec=pltpu.PrefetchScalarGridSpec(
            num_scalar_prefetch=1, grid=(nt,),
            in_specs=[tile(dm), tile(N_EXPERTS), tile(N_EXPERTS), pl.BlockSpec((1, dm), lambda k, b: (0, 0))]
                     + [window(e) for e in range(N_EXPERTS)],
            out_specs=tile(dm)),
        out_shape=jax.ShapeDtypeStruct((n, dm), F32),
        compiler_params=_cparams(1),
        name="moe_combine",
    )(base, h, slot_tm, gate_tm, g, *([outs] * N_EXPERTS))


def _moe(h, norm_g, router_wt, w1, w3, w2, final_g):
    n = h.shape[0]
    cap = CAPACITY_FACTOR * n // N_EXPERTS
    nt = n // MOE_TILE
    hn, aff_t = _router(h, norm_g, router_wt, 512)
    thr = _threshold(aff_t, cap)
    slot, posx = _slots(aff_t, thr, cap)
    base = posx[:, ::MOE_TILE].reshape(N_EXPERTS * nt)
    xs = _gather(hn, slot, base, cap)
    outs = _ffn(xs, w1, w3, w2, cap)
    return _combine(h, slot.T, aff_t.T, base, outs, final_g, cap)


def kernel(x_prompt, x_sample, norm1_g, norm2_g, final_norm_g, w_in, dn_conv_w, dn_a_log, dn_dt_bias, dn_norm_g, dn_proj, s5_lam_re, s5_lam_im, s5_log_step, s5_b_re, s5_b_im, s5_c_re, s5_c_im, s5_d, s5_glu_w, s5_glu_b, s5_proj, rw_mu, rw_w0, rw_w2, rw_a0, rw_a2, rw_g2, rw_k_k, rw_k_a, rw_r_k, rw_ln_g, rw_ln_b, rw_proj, w_out, router_w, expert_w1, expert_w3, expert_w2):
    p = dict(norm1_g=norm1_g, norm2_g=norm2_g, final_norm_g=final_norm_g, w_in=w_in, dn_conv_w=dn_conv_w,
             dn_a_log=dn_a_log, dn_dt_bias=dn_dt_bias, dn_norm_g=dn_norm_g, dn_proj=dn_proj, s5_lam_re=s5_lam_re,
             s5_lam_im=s5_lam_im, s5_log_step=s5_log_step, s5_b_re=s5_b_re, s5_b_im=s5_b_im, s5_c_re=s5_c_re,
             s5_c_im=s5_c_im, s5_d=s5_d, s5_glu_w=s5_glu_w, s5_glu_b=s5_glu_b, s5_proj=s5_proj, rw_mu=rw_mu,
             rw_w0=rw_w0, rw_w2=rw_w2, rw_a0=rw_a0, rw_a2=rw_a2, rw_g2=rw_g2, rw_k_k=rw_k_k, rw_k_a=rw_k_a,
             rw_r_k=rw_r_k, rw_ln_g=rw_ln_g, rw_ln_b=rw_ln_b, rw_proj=rw_proj, w_out=w_out, router_w=router_w,
             expert_w1=expert_w1, expert_w3=expert_w3, expert_w2=expert_w2)
    lp = _layer_params(p, (x_prompt.shape[1], x_sample.shape[1]))
    return _trunk(x_prompt, lp), _trunk(x_sample, lp)


def _layer_params(p, seqlens):
    depth = p['w_in'].shape[0]
    src = _proj_source_columns()
    layers = []
    for layer in range(depth):
        w_in = jnp.where(src[None, :] >= 0, p['w_in'][layer][:, src.clip(0)], 0.0).astype(BF16)
        layers.append({
            'norm1_g': p['norm1_g'][layer], 'norm2_g': p['norm2_g'][layer], 'w_in': w_in,
            'dn': _dn_params(p, layer), 'rw': _rw_params(p, layer), 'merge': _merge_params(p, layer),
            's5': {s: _s5_params(p, layer, s) for s in set(seqlens)},
            'router_wt': p['router_w'][layer].T.astype(BF16),
            'w1': p['expert_w1'][layer].astype(BF16), 'w3': p['expert_w3'][layer].astype(BF16),
            'w2': p['expert_w2'][layer].astype(BF16),
        })
    return {'layers': layers, 'final_norm_g': p['final_norm_g']}


def _trunk(x, lp):
    bsz, seqlen, dm = x.shape
    h = x.reshape(bsz * seqlen, dm)
    tm = min(256, seqlen)
    depth = len(lp['layers'])
    for li, w in enumerate(lp['layers']):
        proj = _norm_proj(h, w['norm1_g'], w['w_in'], min(512, seqlen), 1024)
        q, k, v, bg = _dn_prep(proj, w['dn'], seqlen, tm)
        o_dn = _dn_scan(q, k, v, bg, bsz, seqlen)
        y_s5 = _s5_core(proj, w['s5'][seqlen], bsz, seqlen)
        r, rv, kk, lw, kd, b, bonus, gate_rw = _rw_prep(proj, w['rw'], seqlen, tm)
        y_rw = _rw_scan(r, rv, kk, lw, kd, b, bsz, seqlen)
        h = _merge(h, proj, o_dn, y_s5, y_rw, bonus, gate_rw, w['merge'], tm)
        h = _moe(h, w['norm2_g'], w['router_wt'], w['w1'], w['w3'], w['w2'],
                 lp['final_norm_g'] if li == depth - 1 else None)
    return h.reshape(bsz, seqlen, dm)
```

```python
import functools
import math

import jax
import jax.numpy as jnp
from jax import lax
from jax.experimental import pallas as pl
from jax.experimental.pallas import tpu as pltpu

F32 = jnp.float32
BF16 = jnp.bfloat16
HI = lax.Precision.HIGHEST

D_MODEL = 1024
NORM_EPS = 1e-6
CHUNK = 64
SCAN_LOCKSTEP = 16
DN_HEADS, DN_HEAD_DIM, DN_WIDTH, DN_CONV = 4, 128, 512, 5
S5_WIDTH, S5_GROUP, S5_GROUPS, S5_STATE = 512, 16, 32, 64
RW_WIDTH, RW_HEAD_DIM, RW_HEADS = 512, 64, 8
RW_DECAY_RANK, RW_AAA_RANK, RW_GATE_RANK = 64, 64, 128
RW_GN_EPS = 64e-5
RW_COLS = 3 * RW_WIDTH + 2 * RW_DECAY_RANK + 2 * RW_AAA_RANK + RW_GATE_RANK
N_EXPERTS, EXPERT_FF, CAPACITY_FACTOR = 16, 2048, 2
VMEM_LIMIT = 56 * 1024 * 1024

COL_GATE, COL_QKV, COL_Z, COL_S5, COL_BA, COL_RW, PROJ_COLS = 0, 3072, 4608, 5120, 5632, 6144, 8192
RW_PAD = 2048
HALO = 16


def _proj_source_columns():
    import numpy as np
    w = DN_WIDTH
    off_q, off_z, off_beta = 0, 3 * w, 4 * w
    off_alpha = off_beta + 2 * DN_HEADS
    off_s5 = off_alpha + 2 * DN_HEADS
    off_rw = off_s5 + S5_WIDTH
    off_gate = off_rw + RW_COLS
    src = np.full((PROJ_COLS,), -1, np.int32)
    src[COL_GATE:COL_GATE + 3 * D_MODEL] = off_gate + np.arange(3 * D_MODEL)
    src[COL_QKV:COL_QKV + 3 * w] = off_q + np.arange(3 * w)
    src[COL_Z:COL_Z + w] = off_z + np.arange(w)
    src[COL_S5:COL_S5 + S5_WIDTH] = off_s5 + np.arange(S5_WIDTH)
    lane = np.arange(16)
    src[COL_BA:COL_BA + 16] = np.where((lane & 4) != 0, off_alpha, off_beta) + ((lane >> 3) & 1) * DN_HEADS + (lane & 3)
    src[COL_RW:COL_RW + RW_COLS] = off_rw + np.arange(RW_COLS)
    return src


def _bdot(a, b):
    return jnp.dot(a.astype(BF16), b.astype(BF16), preferred_element_type=F32)


def _bdot_nt(a, b):
    return lax.dot_general(a.astype(BF16), b.astype(BF16), (((1,), (1,)), ((), ())),
                           preferred_element_type=F32)


def _bdot_tn(a, b):
    return lax.dot_general(a.astype(BF16), b.astype(BF16), (((0,), (0,)), ((), ())),
                           preferred_element_type=F32)


def _fdot(a, b):
    return jnp.dot(a, b, precision=HI, preferred_element_type=F32)


def _sigmoid(x):
    return 1.0 / (1.0 + jnp.exp(-x))


def _softplus(x):
    return jnp.maximum(x, 0.0) + jnp.log(1.0 + jnp.exp(-jnp.abs(x)))


def _silu(x):
    return x * _sigmoid(x)


def _cparams(n_axes):
    return pltpu.CompilerParams(dimension_semantics=("arbitrary",) * n_axes, vmem_limit_bytes=VMEM_LIMIT)


def _each(f, *lists):
    return [f(*xs) for xs in zip(*lists)]


def _unit_tri_inverse(ms, n):
    ri = lax.broadcasted_iota(jnp.int32, (n, n), 0)
    ci = lax.broadcasted_iota(jnp.int32, (n, n), 1)
    eye = (ri == ci).astype(F32)
    ts = _each(lambda m: eye - jnp.where((ri >> 1) == (ci >> 1), m, 0.0), ms)
    for lg in range(2, 7):
        couple = ((ri >> lg) == (ci >> lg)) & ((ri >> (lg - 1)) != (ci >> (lg - 1)))
        xs = _each(lambda m, t: _bdot(jnp.where(couple, m, 0.0), t), ms, ts)
        ts = _each(lambda t, x: t - _bdot(t, x), ts, xs)
    return ts


def _norm_proj_kernel(x_ref, g_ref, w_ref, o_ref, xn_ref):
    @pl.when(pl.program_id(1) == 0)
    def _():
        x = x_ref[...]
        y = x * lax.rsqrt(jnp.mean(x * x, axis=-1, keepdims=True) + NORM_EPS)
        xn_ref[...] = (y * g_ref[...]).astype(BF16)

    o_ref[...] = jnp.dot(xn_ref[...], w_ref[...], preferred_element_type=F32).astype(BF16)


def _norm_proj(x, g, w, tm, tn):
    n, d = x.shape
    cols = w.shape[1]
    return pl.pallas_call(
        _norm_proj_kernel,
        grid=(n // tm, cols // tn),
        in_specs=[pl.BlockSpec((tm, d), lambda i, j: (i, 0)),
                  pl.BlockSpec((1, d), lambda i, j: (0, 0)),
                  pl.BlockSpec((d, tn), lambda i, j: (0, j))],
        out_specs=pl.BlockSpec((tm, tn), lambda i, j: (i, j)),
        out_shape=jax.ShapeDtypeStruct((n, cols), BF16),
        scratch_shapes=[pltpu.VMEM((tm, d), BF16)],
        compiler_params=_cparams(2),
        name="norm_proj",
    )(x, g.reshape(1, d), w)


def _rw_prep_kernel(x_ref, xp_ref, xn_ref, mu_ref, w0_ref, w2_ref, a0_ref, a2_ref, g2_ref, kk_w_ref, ka_ref,
                    rk_ref, ones_ref, r_ref, v_ref, kk_ref, lw_ref, kd_ref, b_ref, bonus_ref, gate_ref,
                    *, tiles_per_seq):
    i = pl.program_id(0)
    tm = x_ref.shape[0]
    x = x_ref[...].astype(F32)
    first = (i % tiles_per_seq) == 0
    last = (i % tiles_per_seq) == tiles_per_seq - 1
    row = lax.broadcasted_iota(jnp.int32, x.shape, 0)
    prev_edge = jnp.where(first, 0.0, xp_ref[...].astype(F32)[HALO - 1:HALO, :])
    next_edge = jnp.where(last, 0.0, xn_ref[...].astype(F32)[0:1, :])
    prev = jnp.where(row == 0, prev_edge, pltpu.roll(x, 1, axis=0))
    nxt = jnp.where(row == tm - 1, next_edge, pltpu.roll(x, tm - 1, axis=0))
    f = x + (0.5 * (prev + nxt) - x) * mu_ref[...]
    w = RW_WIDTH
    r, k, v = f[:, 0:w], f[:, w:2 * w], f[:, 2 * w:3 * w]
    w_lo, a_lo, g_lo = f[:, 3 * w:3 * w + 128], f[:, 3 * w + 128:3 * w + 256], f[:, 3 * w + 256:3 * w + 384]
    ones_bd = ones_ref[...]
    kq = k * kk_w_ref[...]
    kk = kq * lax.rsqrt(_fdot(kq * kq, ones_bd) + NORM_EPS)
    gate_ref[...] = _bdot(_sigmoid(g_lo), g2_ref[...]).astype(BF16)
    w_log = -_softplus(-(w0_ref[...] + _bdot(jnp.tanh(w_lo), w2_ref[...]))) - 0.5
    lw_ref[...] = -jnp.exp(w_log)
    a_lr = _sigmoid(a0_ref[...] + _bdot(a_lo, a2_ref[...]))
    ka = ka_ref[...]
    kd0 = k * (1.0 + (a_lr[:, 0:w] - 1.0) * ka)
    kd1 = k * (1.0 + (a_lr[:, w:2 * w] - 1.0) * ka)
    kd_ref[:, 0:w] = kd0.astype(BF16)
    kd_ref[:, w:2 * w] = kd1.astype(BF16)
    b_ref[:, 0:w] = (kk * a_lr[:, 0:w]).astype(BF16)
    b_ref[:, w:2 * w] = (kk * a_lr[:, w:2 * w]).astype(BF16)
    bonus_ref[...] = (_fdot(r * (kd0 + kd1) * rk_ref[...], ones_bd) * v).astype(BF16)
    r_ref[...] = r.astype(BF16)
    v_ref[...] = v.astype(BF16)
    kk_ref[...] = kk.astype(BF16)


def _rw_prep(proj, p, seqlen, tm):
    n = proj.shape[0]
    w = RW_WIDTH
    tps = seqlen // tm
    nhalo = n // HALO
    full = lambda a: pl.BlockSpec(a.shape, lambda i: (0,) * a.ndim)
    consts = [p['mu'], p['w0'], p['w2'], p['a0'], p['a2'], p['g2'], p['k_k'], p['k_a'], p['r_k'], p['ones']]
    out_w = [w, w, w, 2 * w, 2 * w, 2 * w, w, w]
    return pl.pallas_call(
        functools.partial(_rw_prep_kernel, tiles_per_seq=tps),
        grid=(n // tm,),
        in_specs=[pl.BlockSpec((tm, RW_PAD), lambda i: (i, COL_RW // RW_PAD)),
                  pl.BlockSpec((HALO, RW_PAD), lambda i: (jnp.maximum(i * (tm // HALO) - 1, 0), COL_RW // RW_PAD)),
                  pl.BlockSpec((HALO, RW_PAD), lambda i: (jnp.minimum((i + 1) * (tm // HALO), nhalo - 1),
                                                          COL_RW // RW_PAD))]
                 + [full(a) for a in consts],
        out_specs=[pl.BlockSpec((tm, c), lambda i: (i, 0)) for c in out_w],
        out_shape=[jax.ShapeDtypeStruct((n, c), F32 if j == 3 else BF16) for j, c in enumerate(out_w)],
        compiler_params=_cparams(1),
        name="rw_prep",
    )(proj, proj, proj, *consts)


def _scan_masks(n, fwd):
    c = CHUNK
    ri = lax.broadcasted_iota(jnp.int32, (n, n), 0)
    ci = lax.broadcasted_iota(jnp.int32, (n, n), 1)
    ti, tj = ri & (c - 1), ci & (c - 1)
    same_head = (ri >> 6) == (ci >> 6)
    before = (tj < ti) if fwd else (tj > ti)
    strict = same_head & before
    incl = same_head & (before | (ti == tj))
    i64 = lax.broadcasted_iota(jnp.int32, (c, c), 0)
    j64 = lax.broadcasted_iota(jnp.int32, (c, c), 1)
    cum_mask = ((j64 <= i64) if fwd else (j64 >= i64)).astype(F32)
    return strict, incl, cum_mask


def _rw_chunks(fwd, masks, r, v, kk, lw, kd, bb, ht_ref, slots):
    c, pw = CHUNK, 128
    strict, incl = [m[0] for m in masks], [m[1] for m in masks]
    head0 = lax.broadcasted_iota(jnp.int32, (c, pw), 1) < RW_HEAD_DIM
    stack = lambda x: jnp.concatenate([jnp.where(head0, x, 0.0), jnp.where(head0, 0.0, x)], axis=0)
    cat0 = lambda a, b: jnp.concatenate([a, b], axis=0)

    cum = _each(lambda m, x: _fdot(m[2], x), masks, lw)
    tot = _each(lambda x, f: x[c - 1:c, :] if f else x[0:1, :], cum, fwd)
    p_inv = _each(lambda x: jnp.exp(-x), cum)
    p_end_over = _each(lambda t, x: jnp.exp(t - x), tot, cum)
    rd_s = _each(lambda x, cm: stack(x * jnp.exp(cm)), r, cum)
    kp_s = _each(lambda x, cm, l: stack(x * jnp.exp(cm - l)), kk, cum, lw)
    kb_inv = _each(lambda k, b, pi: cat0(stack(k * pi), stack(b * pi)), kd, bb, p_inv)
    kb_end = _each(lambda k, b, pe: cat0(stack(k * pe), stack(b * pe)), kd, bb, p_end_over)
    v_s = _each(stack, v)
    a_k = _each(_bdot_nt, kp_s, kb_inv)
    a_r = _each(_bdot_nt, rd_s, kb_inv)
    t_inv = _unit_tri_inverse(_each(lambda a, st: jnp.where(st, a[:, pw:], 0.0), a_k, strict), pw)
    ht = [ht_ref[s] for s in slots]
    carry = _each(lambda kp, rd, h: _bdot_nt(cat0(kp, rd), h), kp_s, rd_s, ht)
    av = _each(lambda a, x, st: _bdot(jnp.where(st, a[:, :pw], 0.0), x), a_k, v_s, strict)
    u_s = _each(lambda t, cr, x: _bdot(t, cr[:pw] + x), t_inv, carry, av)
    y_s = _each(lambda a, cr, x, u, inc: cr[pw:] + _bdot(
        jnp.concatenate([jnp.where(inc, a[:, :pw], 0.0), -jnp.where(inc, a[:, pw:], 0.0)], axis=1), cat0(x, u)),
        a_r, carry, v_s, u_s, incl)
    upd = _each(lambda x, u, ke: _bdot_tn(cat0(x, -u), ke), v_s, u_s, kb_end)
    for s, h, t, up in zip(slots, ht, tot, upd):
        ht_ref[s] = h * jnp.exp(t) + up
    return _each(lambda y: y[:c] + y[c:], y_s)


def _rw_scan_kernel(*refs, sb):
    ins, (yf_ref, yb_ref, ht_ref) = refs[:12], refs[12:]

    @pl.when(pl.program_id(1) == 0)
    def _():
        ht_ref[...] = jnp.zeros_like(ht_ref)

    npair = RW_WIDTH // 128
    y_refs = (yf_ref, yb_ref)
    both = [_scan_masks(128, True), _scan_masks(128, False)]
    items = [(d, s, p) for d in range(2) for s in range(sb) for p in range(npair)]
    for g in range(0, len(items), SCAN_LOCKSTEP):
        grp = items[g:g + SCAN_LOCKSTEP]
        tiles = [[ins[6 * d + j][s, :, 128 * p:128 * (p + 1)].astype(F32) for d, s, p in grp] for j in range(6)]
        ys = _rw_chunks([d == 0 for d, s, p in grp], [both[d] for d, s, p in grp], *tiles, ht_ref,
                        [(d * sb + s) * npair + p for d, s, p in grp])
        for (d, s, p), y in zip(grp, ys):
            y_refs[d][s, :, 128 * p:128 * (p + 1)] = y.astype(y_refs[d].dtype)


def _scan_batch(bsz):
    return next(s for s in (4, 2, 1) if bsz % s == 0)


def _rw_scan(r, v, kk, lw, kd, b, bsz, seqlen):
    n, w = r.shape
    nc = seqlen // CHUNK
    sb = _scan_batch(bsz)
    r, v, kk = (a.reshape(bsz, seqlen, w) for a in (r, v, kk))
    lw, kd, b = (a.reshape(bsz, seqlen, 2 * w) for a in (lw, kd, b))
    spec = lambda d, col: pl.BlockSpec((sb, CHUNK, w), lambda bi, c: (bi, c + d * (nc - 1 - 2 * c), col))
    yf, yb = pl.pallas_call(
        functools.partial(_rw_scan_kernel, sb=sb),
        grid=(bsz // sb, nc),
        in_specs=[spec(0, 0)] * 6 + [spec(1, 0)] * 3 + [spec(1, 1)] * 3,
        out_specs=[spec(0, 0), spec(1, 0)],
        out_shape=[jax.ShapeDtypeStruct((bsz, seqlen, w), BF16)] * 2,
        scratch_shapes=[pltpu.VMEM((2 * sb * (w // 128), 128, 128), F32)],
        compiler_params=_cparams(2),
        name="rw_scan",
    )(r, v, kk, lw, kd, b, r, v, kk, lw, kd, b)
    return yf.reshape(n, w), yb.reshape(n, w)


def _rw_params(p, layer):
    w = RW_WIDTH
    pad = lambda a: jnp.pad(a, ((0, 0), (0, RW_PAD - a.shape[1])))
    blockdiag2 = lambda m: jnp.concatenate(
        [jnp.concatenate([m[0], jnp.zeros_like(m[0])], axis=1),
         jnp.concatenate([jnp.zeros_like(m[1]), m[1]], axis=1)], axis=0)
    hid = jnp.arange(w) // RW_HEAD_DIM
    return {
        'mu': pad(p['rw_mu'][layer][None, :]),
        'w0': p['rw_w0'][layer].reshape(1, 2 * w),
        'w2': blockdiag2(p['rw_w2'][layer]).astype(BF16),
        'a0': p['rw_a0'][layer].reshape(1, 2 * w),
        'a2': blockdiag2(p['rw_a2'][layer]).astype(BF16),
        'g2': p['rw_g2'][layer].astype(BF16),
        'k_k': p['rw_k_k'][layer][None, :],
        'k_a': p['rw_k_a'][layer][None, :],
        'r_k': p['rw_r_k'][layer].reshape(1, w),
        'ones': (hid[:, None] == hid[None, :]).astype(F32),
    }


def _dn_prep_kernel(x_ref, xp_ref, xn_ref, ba_ref, cw_ref, alog_ref, dtb_ref, ones_ref,
                    q_ref, k_ref, v_ref, bg_ref, *, tiles_per_seq):
    i = pl.program_id(0)
    tm = x_ref.shape[0]
    x = x_ref[...].astype(F32)
    first = (i % tiles_per_seq) == 0
    last = (i % tiles_per_seq) == tiles_per_seq - 1
    row = lax.broadcasted_iota(jnp.int32, x.shape, 0)
    xp = jnp.where(first, 0.0, xp_ref[...].astype(F32))
    xn = jnp.where(last, 0.0, xn_ref[...].astype(F32))
    pad = (DN_CONV - 1) // 2
    acc = x * cw_ref[pad:pad + 1, :]
    for s in range(1, pad + 1):
        back = pltpu.roll(x, s, axis=0)
        fore = pltpu.roll(x, tm - s, axis=0)
        for t in range(s):
            back = jnp.where(row == t, xp[HALO - s + t:HALO + 1 - s + t, :], back)
            fore = jnp.where(row == tm - 1 - t, xn[s - 1 - t:s - t, :], fore)
        acc = acc + back * cw_ref[pad - s:pad - s + 1, :] + fore * cw_ref[pad + s:pad + s + 1, :]
    y = _silu(acc)
    w = DN_WIDTH
    ones_bd = ones_ref[...]
    q, k = y[:, 0:w], y[:, w:2 * w]
    q_ref[...] = (q * (lax.rsqrt(_fdot(q * q, ones_bd) + NORM_EPS) * (DN_HEAD_DIM ** -0.5))).astype(BF16)
    k_ref[...] = (k * lax.rsqrt(_fdot(k * k, ones_bd) + NORM_EPS)).astype(BF16)
    v_ref[...] = y[:, 2 * w:3 * w].astype(BF16)
    ba = ba_ref[...].astype(F32)
    lane = lax.broadcasted_iota(jnp.int32, ba.shape, 1)
    g = -jnp.exp(alog_ref[...]) * _softplus(ba + dtb_ref[...])
    bg_ref[...] = jnp.where((lane & 4) == 0, _sigmoid(ba), g)


def _dn_prep(proj, p, seqlen, tm):
    n = proj.shape[0]
    w = DN_WIDTH
    tps = seqlen // tm
    nhalo = n // HALO
    full = lambda a: pl.BlockSpec(a.shape, lambda i: (0,) * a.ndim)
    consts = [p['conv_w'], p['a_log'], p['dt_bias'], p['ones']]
    cb = COL_QKV // (3 * w)
    return pl.pallas_call(
        functools.partial(_dn_prep_kernel, tiles_per_seq=tps),
        grid=(n // tm,),
        in_specs=[pl.BlockSpec((tm, 3 * w), lambda i: (i, cb)),
                  pl.BlockSpec((HALO, 3 * w), lambda i: (jnp.maximum(i * (tm // HALO) - 1, 0), cb)),
                  pl.BlockSpec((HALO, 3 * w), lambda i: (jnp.minimum((i + 1) * (tm // HALO), nhalo - 1), cb)),
                  pl.BlockSpec((tm, 128), lambda i: (i, COL_BA // 128))]
                 + [full(a) for a in consts],
        out_specs=[pl.BlockSpec((tm, c), lambda i: (i, 0)) for c in (w, w, w, 128)],
        out_shape=[jax.ShapeDtypeStruct((n, c), F32 if c == 128 else BF16) for c in (w, w, w, 128)],
        compiler_params=_cparams(1),
        name="dn_prep",
    )(proj, proj, proj, proj, *consts)


def _dn_chunks(fwd, masks, q, k, v, bg, s_ref, slots):
    c, hd, nh = CHUNK, DN_HEAD_DIM, DN_HEADS
    n = nh * c
    heads = range(nh)
    strict, incl = [m[0] for m in masks], [m[1] for m in masks]
    lane0 = [0 if f else 8 for f in fwd]
    gcum_all = _each(lambda m, x: _fdot(m[2], x), masks, bg)
    bgd = _each(lambda x, l: x[:, l:l + 8], bg, lane0)
    gcum = _each(lambda x, l: x[:, l:l + 8], gcum_all, lane0)
    gtot = _each(lambda x, f: x[c - 1:c, :] if f else x[0:1, :], gcum, fwd)
    beta_s = _each(lambda x: jnp.concatenate([x[:, h:h + 1] for h in heads], axis=0), bgd)
    gc_s = _each(lambda x: jnp.concatenate([x[:, nh + h:nh + h + 1] for h in heads], axis=0), gcum)
    gend_s = _each(lambda x: jnp.concatenate(
        [jnp.broadcast_to(x[:, nh + h:nh + h + 1], (c, 1)) for h in heads], axis=0), gtot)
    stack = lambda x: jnp.concatenate([x[:, hd * h:hd * (h + 1)] for h in heads], axis=0)
    q_s, k_s, v_s = _each(stack, q), _each(stack, k), _each(stack, v)

    def decay_of(g, inc):
        rows = jnp.broadcast_to(g, (n, n))
        return jnp.exp(jnp.where(inc, rows - rows.T, -1e30))

    decay = _each(decay_of, gc_s, incl)
    kb_s = _each(lambda x, b: x * b, k_s, beta_s)
    qk = _each(lambda kb, x, kk: _bdot_nt(jnp.concatenate([kb, x], axis=0), kk), kb_s, q_s, k_s)
    t_inv = _unit_tri_inverse(_each(lambda a, dc, st: jnp.where(st, a[:n] * dc, 0.0), qk, decay, strict), n)
    a_qk = _each(lambda a, dc, inc: jnp.where(inc, a[n:] * dc, 0.0), qk, decay, incl)
    e_gc = _each(jnp.exp, gc_s)
    uw = _each(lambda t, x, b, kb, e: _bdot(t, jnp.concatenate([x * b, kb * e], axis=1)),
               t_inv, v_s, beta_s, kb_s, e_gc)
    q_dec = _each(lambda x, e: x * e, q_s, e_gc)
    k_dec = _each(lambda x, ge, g: x * jnp.exp(ge - g), k_s, gend_s, gc_s)
    e_end = _each(jnp.exp, gtot)
    rows = [slice(c * h, c * (h + 1)) for h in heads]
    states = [[s_ref[sl + h] for h in heads] for sl in slots]
    carry = _each(lambda x, qd, st: [_bdot(jnp.concatenate([x[rows[h], hd:], qd[rows[h]]], axis=0), st[h])
                                     for h in heads], uw, q_dec, states)
    v_new = _each(lambda x, cr: x[:, :hd] - jnp.concatenate([cr[h][:c] for h in heads], axis=0), uw, carry)
    o_s = _each(lambda cr, a, x: jnp.concatenate([cr[h][c:] for h in heads], axis=0) + _bdot(a, x),
                carry, a_qk, v_new)
    upd = _each(lambda kd, x: [_bdot_tn(kd[rows[h]], x[rows[h]]) for h in heads], k_dec, v_new)
    for sl, st, e, up in zip(slots, states, e_end, upd):
        for h in heads:
            s_ref[sl + h] = st[h] * e[:, nh + h:nh + h + 1] + up[h]
    return _each(lambda x: jnp.concatenate([x[rows[h]] for h in heads], axis=1), o_s)


def _dn_scan_kernel(*refs, sb):
    ins, outs, s_ref = refs[:8], refs[8:10], refs[10]

    @pl.when(pl.program_id(1) == 0)
    def _():
        s_ref[...] = jnp.zeros_like(s_ref)

    items = [(d, s) for d in range(2) for s in range(sb)]
    both = [_scan_masks(DN_HEADS * CHUNK, True), _scan_masks(DN_HEADS * CHUNK, False)]
    tiles = [[ins[4 * d + j][s].astype(F32) for d, s in items] for j in range(4)]
    os_ = _dn_chunks([d == 0 for d, s in items], [both[d] for d, s in items], *tiles, s_ref,
                     [(d * sb + s) * DN_HEADS for d, s in items])
    for (d, s), o in zip(items, os_):
        outs[d][s] = o.astype(outs[d].dtype)


def _dn_scan(q, k, v, bg, bsz, seqlen):
    n, w = q.shape
    nc = seqlen // CHUNK
    sb = _scan_batch(bsz)
    q, k, v = (a.reshape(bsz, seqlen, w) for a in (q, k, v))
    bg = bg.reshape(bsz, seqlen, 128)
    spec = lambda d, cols: pl.BlockSpec((sb, CHUNK, cols), lambda bi, c: (bi, c + d * (nc - 1 - 2 * c), 0))
    dir_specs = lambda d: [spec(d, w)] * 3 + [spec(d, 128)]
    of, ob = pl.pallas_call(
        functools.partial(_dn_scan_kernel, sb=sb),
        grid=(bsz // sb, nc),
        in_specs=dir_specs(0) + dir_specs(1),
        out_specs=[spec(0, w), spec(1, w)],
        out_shape=[jax.ShapeDtypeStruct((bsz, seqlen, w), BF16)] * 2,
        scratch_shapes=[pltpu.VMEM((2 * sb * DN_HEADS, DN_HEAD_DIM, DN_HEAD_DIM), F32)],
        compiler_params=_cparams(2),
        name="dn_scan",
    )(q, k, v, bg, q, k, v, bg)
    return of.reshape(n, w), ob.reshape(n, w)


def _dn_params(p, layer):
    lane = jnp.arange(128)
    dirn, hh, is_alpha = (lane >> 3) & 1, lane & 3, ((lane & 4) != 0) & (lane < 16)
    a_log = jnp.where(is_alpha, p['dn_a_log'][layer][dirn, hh], 0.0)
    dt_bias = jnp.where(is_alpha, p['dn_dt_bias'][layer][dirn, hh], 0.0)
    hid = jnp.arange(DN_WIDTH) // DN_HEAD_DIM
    return {
        'conv_w': jnp.pad(p['dn_conv_w'][layer], ((0, 8 - DN_CONV), (0, 0))),
        'a_log': a_log[None, :].astype(F32),
        'dt_bias': dt_bias[None, :].astype(F32),
        'ones': (hid[:, None] == hid[None, :]).astype(F32),
    }


S5_CHUNK = 16
S5_FLAT = S5_CHUNK * S5_GROUP


def _s5_kernel(u_ref, ws_ref, wy_ref, ca_ref, cb_ref, y_ref, *, levels):
    u = u_ref[...]
    nc = u.shape[0]
    x = _bdot(u, ws_ref[...])
    row = lax.broadcasted_iota(jnp.int32, (nc, 128), 0)
    ca, cb = ca_ref[...], cb_ref[...]

    def cmul(k, half, h):
        a = ca[k:k + 1, 128 * half:128 * (half + 1)]
        b = cb[k:k + 1, 128 * half:128 * (half + 1)]
        return a * h + b * pltpu.roll(h, S5_STATE, axis=1)

    hf, hb = x[:, 0:128], x[:, 128:256]
    for k in range(levels):
        s = 1 << k
        hf = hf + jnp.where(row >= s, cmul(k, 0, pltpu.roll(hf, s, axis=0)), 0.0)
        hb = hb + jnp.where(row < nc - s, cmul(k, 1, pltpu.roll(hb, nc - s, axis=0)), 0.0)
    hf = jnp.where(row >= 1, pltpu.roll(hf, 1, axis=0), 0.0)
    hb = jnp.where(row < nc - 1, pltpu.roll(hb, nc - 1, axis=0), 0.0)
    y_ref[...] = _bdot(jnp.concatenate([u.astype(BF16), hf.astype(BF16), hb.astype(BF16)], axis=1),
                       wy_ref[...]).astype(BF16)


def _s5_core(proj, sp, bsz, seqlen):
    n = proj.shape[0]
    nc = seqlen // S5_CHUNK
    u = lax.slice_in_dim(proj, COL_S5, COL_S5 + S5_WIDTH, axis=1)
    ug = u.reshape(bsz, nc, S5_CHUNK, S5_GROUPS, S5_GROUP).transpose(3, 0, 1, 2, 4).reshape(S5_GROUPS, bsz * nc, S5_FLAT)
    levels = sp['ca'].shape[1]
    wspec = lambda a: pl.BlockSpec((None,) + a.shape[1:], lambda g, b: (g, 0, 0))
    yg = pl.pallas_call(
        functools.partial(_s5_kernel, levels=levels),
        grid=(S5_GROUPS, bsz),
        in_specs=[pl.BlockSpec((None, nc, S5_FLAT), lambda g, b: (g, b, 0)),
                  wspec(sp['ws']), wspec(sp['wy']), wspec(sp['ca']), wspec(sp['cb'])],
        out_specs=pl.BlockSpec((None, nc, S5_FLAT), lambda g, b: (g, b, 0)),
        out_shape=jax.ShapeDtypeStruct((S5_GROUPS, bsz * nc, S5_FLAT), BF16),
        compiler_params=_cparams(2),
        name="s5_core",
    )(ug, sp['ws'], sp['wy'], sp['ca'], sp['cb'])
    return yg.reshape(S5_GROUPS, bsz, nc, S5_CHUNK, S5_GROUP).transpose(1, 2, 3, 0, 4).reshape(n, S5_WIDTH)


def _s5_params(p, layer, seqlen):
    cs, gs, ps = S5_CHUNK, S5_GROUP, S5_STATE
    cexp = lambda zr, zi: (jnp.exp(zr) * jnp.cos(zi), jnp.exp(zr) * jnp.sin(zi))
    cmul = lambda a, b: (a[0] * b[0] - a[1] * b[1], a[0] * b[1] + a[1] * b[0])
    lam_re, lam_im = p['s5_lam_re'][layer].astype(F32), p['s5_lam_im'][layer].astype(F32)
    step = jnp.exp(p['s5_log_step'][layer].astype(F32))[..., None]
    zr, zi = lam_re * step, lam_im * step
    lbar_re, lbar_im = cexp(zr, zi)
    den = lam_re * lam_re + lam_im * lam_im
    num_re = lbar_re - 1.0
    f = ((num_re * lam_re + lbar_im * lam_im) / den, (lbar_im * lam_re - num_re * lam_im) / den)
    bbar = cmul((f[0][..., None], f[1][..., None]),
                (p['s5_b_re'][layer].astype(F32), p['s5_b_im'][layer].astype(F32)))
    cc = (p['s5_c_re'][layer].astype(F32), p['s5_c_im'][layer].astype(F32))
    tau = jnp.arange(cs + 1, dtype=F32)
    pw = cexp(zr[..., None] * tau, zi[..., None] * tau)
    m = cmul((pw[0][..., None], pw[1][..., None]), (bbar[0][..., None, :], bbar[1][..., None, :]))
    kern = (jnp.einsum('dgcp,dgpte->dgtce', cc[0], m[0], precision=HI)
            - jnp.einsum('dgcp,dgpte->dgtce', cc[1], m[1], precision=HI))
    s_i, t_i = jnp.arange(cs)[:, None], jnp.arange(cs)[None, :]
    kf = jnp.where((t_i >= s_i)[None, :, :, None, None], kern[0][:, jnp.clip(t_i - s_i, 0, cs)], 0.0)
    kb = jnp.where((s_i >= t_i)[None, :, :, None, None], kern[1][:, jnp.clip(s_i - t_i, 0, cs)], 0.0)
    tmat = (kf + kb).transpose(0, 1, 4, 2, 3).reshape(S5_GROUPS, cs * gs, cs * gs)
    sidx = jnp.arange(cs)
    sel = lambda d, idx: (pw[0][d][..., idx][..., None], pw[1][d][..., idx][..., None])
    inj = lambda d: (bbar[0][d][:, :, None, :], bbar[1][d][:, :, None, :])
    ws_f = cmul(sel(0, cs - 1 - sidx), inj(0))
    ws_b = cmul(sel(1, sidx), inj(1))
    to_cols = lambda w: jnp.concatenate([w[0], w[1]], axis=1).transpose(0, 2, 3, 1).reshape(
        S5_GROUPS, cs * gs, 2 * ps)
    ws = jnp.concatenate([to_cols(ws_f), to_cols(ws_b)], axis=2)
    out = lambda d: (cc[0][d].transpose(0, 2, 1)[:, :, None, :], cc[1][d].transpose(0, 2, 1)[:, :, None, :])
    wo_f = cmul(out(0), sel(0, 1 + sidx))
    wo_b = cmul(out(1), sel(1, cs - sidx))
    to_rows = lambda w: jnp.concatenate([w[0], -w[1]], axis=1).reshape(S5_GROUPS, 2 * ps, cs * gs)
    wy = jnp.concatenate([tmat, to_rows(wo_f), to_rows(wo_b)], axis=1)
    levels = max(1, (seqlen // cs - 1).bit_length())
    span = cs * (2.0 ** jnp.arange(levels, dtype=F32))
    lk = cexp(zr[..., None] * span, zi[..., None] * span)
    re, im = lk[0].transpose(0, 1, 3, 2), lk[1].transpose(0, 1, 3, 2)
    ca = jnp.concatenate([re[0], re[0], re[1], re[1]], axis=-1)
    cb = jnp.concatenate([-im[0], im[0], -im[1], im[1]], axis=-1)
    return {'ws': ws.astype(BF16), 'wy': wy.astype(BF16), 'ca': ca.astype(F32), 'cb': cb.astype(F32)}


def _group_sum(x, ones_bf16):
    hi = x.astype(BF16)
    lo = (x - hi.astype(F32)).astype(BF16)
    return (jnp.dot(hi, ones_bf16, preferred_element_type=F32) + jnp.dot(lo, ones_bf16, preferred_element_type=F32))


def _gelu_tanh(x):
    return 0.5 * x * (1.0 + jnp.tanh(math.sqrt(2.0 / math.pi) * (x + 0.044715 * (x * x * x))))


def _merge_kernel(h_ref, gate_ref, z_ref, u_ref, odnf_ref, odnb_ref, ys5_ref, yrwf_ref, yrwb_ref, bonus_ref,
                  grw_ref, ones128_ref, ones64_ref, dng_ref, s5d_ref, glub_ref, lng_ref, lnb_ref,
                  gluw_ref, dnp_ref, s5p_ref, rwp_ref, wout_ref, o_ref):
    f32 = lambda ref: ref[...].astype(F32)
    o = f32(odnf_ref) + f32(odnb_ref)
    ms = _group_sum(o * o, ones128_ref[...]) * (1.0 / DN_HEAD_DIM)
    o_a = o * lax.rsqrt(ms + NORM_EPS) * dng_ref[...] * _silu(f32(z_ref))
    y = _gelu_tanh(f32(u_ref) * s5d_ref[...] + f32(ys5_ref))
    o_b = y * _sigmoid(_bdot(y, gluw_ref[...]) + glub_ref[...])
    yr = f32(yrwf_ref) + f32(yrwb_ref)
    ones64 = ones64_ref[...]
    cen = yr - _group_sum(yr, ones64) * (1.0 / RW_HEAD_DIM)
    var = _group_sum(cen * cen, ones64) * (1.0 / RW_HEAD_DIM)
    o_c = (cen * lax.rsqrt(var + RW_GN_EPS) * lng_ref[...] + lnb_ref[...] + f32(bonus_ref)) * f32(grw_ref)
    dm = D_MODEL
    gate = lambda j: _sigmoid(gate_ref[:, j * dm:(j + 1) * dm].astype(F32))
    merged = (gate(0) * _bdot(o_a, dnp_ref[...]) + gate(1) * _bdot(o_b, s5p_ref[...])
              + gate(2) * _bdot(o_c, rwp_ref[...]))
    o_ref[...] = h_ref[...] + _bdot(merged, wout_ref[...])


def _merge(h, proj, o_dn, y_s5, y_rw, bonus, gate_rw, mp, tm):
    n, dm = h.shape
    w = DN_WIDTH
    row = lambda cols, cb=0: pl.BlockSpec((tm, cols), lambda i: (i, cb))
    full = lambda a: pl.BlockSpec(a.shape, lambda i: (0,) * a.ndim)
    consts = [mp['ones128'], mp['ones64'], mp['dn_norm_g'], mp['s5_d'], mp['glu_b'], mp['ln_g'], mp['ln_b'],
              mp['glu_w'], mp['dn_proj'], mp['s5_proj'], mp['rw_proj'], mp['w_out']]
    return pl.pallas_call(
        _merge_kernel,
        grid=(n // tm,),
        in_specs=[row(dm), row(3 * dm, COL_GATE // (3 * dm)), row(w, COL_Z // w), row(w, COL_S5 // w),
                  row(w), row(w), row(w), row(w), row(w), row(w), row(w)] + [full(a) for a in consts],
        out_specs=row(dm),
        out_shape=jax.ShapeDtypeStruct((n, dm), F32),
        compiler_params=_cparams(1),
        name="merge",
    )(h, proj, proj, proj, o_dn[0], o_dn[1], y_s5, y_rw[0], y_rw[1], bonus, gate_rw, *consts)


def _merge_params(p, layer):
    hid128 = jnp.arange(DN_WIDTH) // DN_HEAD_DIM
    hid64 = jnp.arange(RW_WIDTH) // RW_HEAD_DIM
    r1 = lambda a: a.reshape(1, -1).astype(F32)
    return {
        'ones128': (hid128[:, None] == hid128[None, :]).astype(BF16),
        'ones64': (hid64[:, None] == hid64[None, :]).astype(BF16),
        'dn_norm_g': r1(jnp.tile(p['dn_norm_g'][layer], DN_HEADS)),
        's5_d': r1(p['s5_d'][layer]), 'glu_b': r1(p['s5_glu_b'][layer]),
        'ln_g': r1(p['rw_ln_g'][layer]), 'ln_b': r1(p['rw_ln_b'][layer]),
        'glu_w': p['s5_glu_w'][layer].astype(BF16), 'dn_proj': p['dn_proj'][layer].astype(BF16),
        's5_proj': p['s5_proj'][layer].astype(BF16), 'rw_proj': p['rw_proj'][layer].astype(BF16),
        'w_out': p['w_out'][layer].astype(BF16),
    }


MOE_TILE = 128
MOE_WIN = MOE_TILE + 16
MOE_GATHER_TILES = 16
MOE_ROWS = 512


def _slot_rows(cap):
    tr = min(MOE_ROWS, cap)
    return -(-(cap + MOE_WIN) // tr) * tr


def _router_kernel(h_ref, g_ref, rw_ref, hn_ref, aff_ref):
    x = h_ref[...]
    hn = (x * lax.rsqrt(jnp.mean(x * x, axis=-1, keepdims=True) + NORM_EPS) * g_ref[...]).astype(BF16)
    hn_ref[...] = hn
    logits = lax.dot_general(rw_ref[...], hn, (((1,), (1,)), ((), ())), preferred_element_type=F32)
    ex = jnp.exp(logits - jnp.max(logits, axis=0, keepdims=True))
    aff_ref[...] = ex / jnp.sum(ex, axis=0, keepdims=True)


def _router(h, g, router_wt, tm):
    n, dm = h.shape
    return pl.pallas_call(
        _router_kernel,
        grid=(n // tm,),
        in_specs=[pl.BlockSpec((tm, dm), lambda i: (i, 0)), pl.BlockSpec((1, dm), lambda i: (0, 0)),
                  pl.BlockSpec((N_EXPERTS, dm), lambda i: (0, 0))],
        out_specs=[pl.BlockSpec((tm, dm), lambda i: (i, 0)), pl.BlockSpec((N_EXPERTS, tm), lambda i: (0, i))],
        out_shape=[jax.ShapeDtypeStruct((n, dm), BF16), jax.ShapeDtypeStruct((N_EXPERTS, n), F32)],
        compiler_params=_cparams(1),
        name="router",
    )(h, g.reshape(1, dm), router_wt)


def _threshold_kernel(aff_ref, thr_ref, *, cap):
    bits = pltpu.bitcast(aff_ref[...], jnp.int32)

    def body(i, t):
        cand = t | (jnp.int32(1) << (30 - i))
        cnt = jnp.sum((bits >= cand).astype(F32), axis=1, keepdims=True)
        return jnp.where(cnt >= cap, cand, t)

    t = lax.fori_loop(0, 31, body, jnp.zeros((N_EXPERTS, 1), jnp.int32))
    thr_ref[...] = jnp.broadcast_to(t, thr_ref.shape)


def _threshold(aff_t, cap):
    e, n = aff_t.shape
    return pl.pallas_call(
        functools.partial(_threshold_kernel, cap=cap),
        grid=(1,),
        in_specs=[pl.BlockSpec((e, n), lambda i: (0, 0))],
        out_specs=pl.BlockSpec((e, 128), lambda i: (0, 0)),
        out_shape=jax.ShapeDtypeStruct((e, 128), jnp.int32),
        compiler_params=_cparams(1),
        name="moe_threshold",
    )(aff_t)


def _slots_kernel(thr_ref, aff_ref, slot_ref, posx_ref, *, cap):
    e = pl.program_id(0)
    thr = thr_ref[e]
    bits = pltpu.bitcast(aff_ref[...], jnp.int32)
    nb = bits.shape[0]
    li = lax.broadcasted_iota(jnp.int32, (128, 128), 0)
    lj = lax.broadcasted_iota(jnp.int32, (128, 128), 1)
    upper = (li <= lj).astype(BF16)
    bi = lax.broadcasted_iota(jnp.int32, (nb, nb), 0)
    bj = lax.broadcasted_iota(jnp.int32, (nb, nb), 1)
    lower_strict = (bj < bi).astype(BF16)

    def exclusive_count(m):
        incl = jnp.dot(m.astype(BF16), upper, preferred_element_type=F32)
        tot = jnp.broadcast_to(incl[:, 127:128], (nb, 128))
        offs = jnp.dot(lower_strict, tot.astype(BF16), preferred_element_type=F32)
        return offs + incl - m

    gt = (bits > thr).astype(F32)
    eq = (bits == thr).astype(F32)
    need = cap - jnp.sum(jnp.sum(gt, axis=1, keepdims=True), axis=0, keepdims=True)
    sel = jnp.maximum(gt, eq * (exclusive_count(eq) < need).astype(F32))
    pos = exclusive_count(sel).astype(jnp.int32)
    posx_ref[...] = pos
    slot_ref[...] = jnp.where(sel > 0.0, pos, -1)


def _slots(aff_t, thr, cap):
    e, n = aff_t.shape
    nb = n // 128
    blk = pl.BlockSpec((nb, 128), lambda i, thr_ref: (i, 0))
    slot, posx = pl.pallas_call(
        functools.partial(_slots_kernel, cap=cap),
        grid_spec=pltpu.PrefetchScalarGridSpec(
            num_scalar_prefetch=1, grid=(e,), in_specs=[blk], out_specs=[blk, blk]),
        out_shape=[jax.ShapeDtypeStruct((e * nb, 128), jnp.int32)] * 2,
        compiler_params=_cparams(1),
        name="moe_slots",
    )(thr[:, 0], aff_t.reshape(e * nb, 128))
    return slot.reshape(e, n), posx.reshape(e, n)


def _gather_kernel(base_ref, slot_ref, hn_ref, xs_ref, *, nsub):
    e, k = pl.program_id(0), pl.program_id(1)

    @pl.when(k == 0)
    def _():
        xs_ref[...] = jnp.zeros_like(xs_ref)

    tile0 = (e * pl.num_programs(1) + k) * nsub
    srow = lax.broadcasted_iota(jnp.int32, (MOE_WIN, MOE_TILE), 0)

    def body(j, carry):
        start = pl.multiple_of((base_ref[tile0 + j] >> 4) << 4, 16)
        onehot = (slot_ref[pl.ds(j, 1), :] - start == srow).astype(BF16)
        tok = hn_ref[pl.ds(pl.multiple_of(j * MOE_TILE, MOE_TILE), MOE_TILE), :]
        rows = jnp.dot(onehot, tok, preferred_element_type=F32).astype(BF16)
        win = pl.ds(start, MOE_WIN)
        xs_ref[win, :] = xs_ref[win, :] + rows
        return carry

    lax.fori_loop(0, nsub, body, 0, unroll=min(4, nsub))


def _gather(hn, slot, base, cap):
    n, dm = hn.shape
    nsub = min(MOE_GATHER_TILES, n // MOE_TILE)
    ngs = n // (nsub * MOE_TILE)
    rows = _slot_rows(cap)
    return pl.pallas_call(
        functools.partial(_gather_kernel, nsub=nsub),
        grid_spec=pltpu.PrefetchScalarGridSpec(
            num_scalar_prefetch=1, grid=(N_EXPERTS, ngs),
            in_specs=[pl.BlockSpec((None, nsub, MOE_TILE), lambda e, k, b: (e * ngs + k, 0, 0)),
                      pl.BlockSpec((nsub * MOE_TILE, dm), lambda e, k, b: (k, 0))],
            out_specs=pl.BlockSpec((None, rows, dm), lambda e, k, b: (e, 0, 0))),
        out_shape=jax.ShapeDtypeStruct((N_EXPERTS, rows, dm), BF16),
        compiler_params=_cparams(2),
        name="moe_gather",
    )(base, slot.reshape(N_EXPERTS * ngs, nsub, MOE_TILE), hn)


def _ffn_kernel(x_ref, w1_ref, w3_ref, w2_ref, o_ref):
    x = x_ref[...]
    a = jnp.dot(x, w1_ref[...], preferred_element_type=F32)
    g = jnp.dot(x, w3_ref[...], preferred_element_type=F32)
    o_ref[...] = jnp.dot((_silu(a) * g).astype(BF16), w2_ref[...], preferred_element_type=F32).astype(BF16)


def _ffn(xs, w1, w3, w2, cap):
    e, rows, dm = xs.shape
    ff = w1.shape[2]
    tr = min(MOE_ROWS, cap)
    return pl.pallas_call(
        _ffn_kernel,
        grid=(e, rows // tr),
        in_specs=[pl.BlockSpec((None, tr, dm), lambda i, r: (i, r, 0)),
                  pl.BlockSpec((None, dm, ff), lambda i, r: (i, 0, 0)),
                  pl.BlockSpec((None, dm, ff), lambda i, r: (i, 0, 0)),
                  pl.BlockSpec((None, ff, dm), lambda i, r: (i, 0, 0))],
        out_specs=pl.BlockSpec((None, tr, dm), lambda i, r: (i, r, 0)),
        out_shape=jax.ShapeDtypeStruct((e, rows, dm), BF16),
        compiler_params=_cparams(2),
        name="moe_ffn",
    )(xs, w1, w3, w2)


def _combine_kernel(base_ref, h_ref, slot_ref, gate_ref, g_ref, *rest, final_norm):
    wins, o_ref = rest[:N_EXPERTS], rest[N_EXPERTS]
    k = pl.program_id(0)
    nt = pl.num_programs(0)
    scol = lax.broadcasted_iota(jnp.int32, (MOE_TILE, MOE_WIN), 1)
    slot, gate = slot_ref[...], gate_ref[...]
    acc = h_ref[...]
    for e in range(N_EXPERTS):
        start = (base_ref[e * nt + k] >> 4) << 4
        pick = jnp.where(slot[:, e:e + 1] - start == scol, gate[:, e:e + 1], 0.0).astype(BF16)
        acc = acc + jnp.dot(pick, wins[e][...], preferred_element_type=F32)
    if final_norm:
        acc = acc * lax.rsqrt(jnp.mean(acc * acc, axis=-1, keepdims=True) + NORM_EPS) * g_ref[...]
    o_ref[...] = acc


def _combine(h, slot_tm, gate_tm, base, outs, final_g, cap):
    n, dm = h.shape
    nt = n // MOE_TILE
    rows = outs.shape[1]
    outs = outs.reshape(N_EXPERTS * rows, dm)
    tile = lambda cols: pl.BlockSpec((MOE_TILE, cols), lambda k, b: (k, 0))

    def window(e):
        return pl.BlockSpec((pl.Element(MOE_WIN), pl.Element(dm)),
                            lambda k, b: (pl.multiple_of(e * rows + ((b[e * nt + k] >> 4) << 4), 16), 0))

    g = jnp.ones((1, dm), F32) if final_g is None else final_g.reshape(1, dm).astype(F32)
    return pl.pallas_call(
        functools.partial(_combine_kernel, final_norm=final_g is not None),
        grid_spec=pltpu.PrefetchScalarGridSpec(
            num_scalar_prefetch=1, grid=(nt,),
            in_specs=[tile(dm), tile(N_EXPERTS), tile(N_EXPERTS), pl.BlockSpec((1, dm), lambda k, b: (0, 0))]
                     + [window(e) for e in range(N_EXPERTS)],
            out_specs=tile(dm)),
        out_shape=jax.ShapeDtypeStruct((n, dm), F32),
        compiler_params=_cparams(1),
        name="moe_combine",
    )(base, h, slot_tm, gate_tm, g, *([outs] * N_EXPERTS))


def _moe(h, norm_g, router_wt, w1, w3, w2, final_g):
    n = h.shape[0]
    cap = CAPACITY_FACTOR * n // N_EXPERTS
    nt = n // MOE_TILE
    hn, aff_t = _router(h, norm_g, router_wt, 512)
    thr = _threshold(aff_t, cap)
    slot, posx = _slots(aff_t, thr, cap)
    base = posx[:, ::MOE_TILE].reshape(N_EXPERTS * nt)
    xs = _gather(hn, slot, base, cap)
    outs = _ffn(xs, w1, w3, w2, cap)
    return _combine(h, slot.T, aff_t.T, base, outs, final_g, cap)


def kernel(x_prompt, x_sample, norm1_g, norm2_g, final_norm_g, w_in, dn_conv_w, dn_a_log, dn_dt_bias, dn_norm_g, dn_proj, s5_lam_re, s5_lam_im, s5_log_step, s5_b_re, s5_b_im, s5_c_re, s5_c_im, s5_d, s5_glu_w, s5_glu_b, s5_proj, rw_mu, rw_w0, rw_w2, rw_a0, rw_a2, rw_g2, rw_k_k, rw_k_a, rw_r_k, rw_ln_g, rw_ln_b, rw_proj, w_out, router_w, expert_w1, expert_w3, expert_w2):
    p = dict(norm1_g=norm1_g, norm2_g=norm2_g, final_norm_g=final_norm_g, w_in=w_in, dn_conv_w=dn_conv_w,
             dn_a_log=dn_a_log, dn_dt_bias=dn_dt_bias, dn_norm_g=dn_norm_g, dn_proj=dn_proj, s5_lam_re=s5_lam_re,
             s5_lam_im=s5_lam_im, s5_log_step=s5_log_step, s5_b_re=s5_b_re, s5_b_im=s5_b_im, s5_c_re=s5_c_re,
             s5_c_im=s5_c_im, s5_d=s5_d, s5_glu_w=s5_glu_w, s5_glu_b=s5_glu_b, s5_proj=s5_proj, rw_mu=rw_mu,
             rw_w0=rw_w0, rw_w2=rw_w2, rw_a0=rw_a0, rw_a2=rw_a2, rw_g2=rw_g2, rw_k_k=rw_k_k, rw_k_a=rw_k_a,
             rw_r_k=rw_r_k, rw_ln_g=rw_ln_g, rw_ln_b=rw_ln_b, rw_proj=rw_proj, w_out=w_out, router_w=router_w,
             expert_w1=expert_w1, expert_w3=expert_w3, expert_w2=expert_w2)
    lp = _layer_params(p, (x_prompt.shape[1], x_sample.shape[1]))
    return _trunk(x_prompt, lp), _trunk(x_sample, lp)


def _layer_params(p, seqlens):
    depth = p['w_in'].shape[0]
    src = _proj_source_columns()
    layers = []
    for layer in range(depth):
        w_in = jnp.where(src[None, :] >= 0, p['w_in'][layer][:, src.clip(0)], 0.0).astype(BF16)
        layers.append({
            'norm1_g': p['norm1_g'][layer], 'norm2_g': p['norm2_g'][layer], 'w_in': w_in,
            'dn': _dn_params(p, layer), 'rw': _rw_params(p, layer), 'merge': _merge_params(p, layer),
            's5': {s: _s5_params(p, layer, s) for s in set(seqlens)},
            'router_wt': p['router_w'][layer].T.astype(BF16),
            'w1': p['expert_w1'][layer].astype(BF16), 'w3': p['expert_w3'][layer].astype(BF16),
            'w2': p['expert_w2'][layer].astype(BF16),
        })
    return {'layers': layers, 'final_norm_g': p['final_norm_g']}


def _trunk(x, lp):
    bsz, seqlen, dm = x.shape
    h = x.reshape(bsz * seqlen, dm)
    tm = min(256, seqlen)
    depth = len(lp['layers'])
    for li, w in enumerate(lp['layers']):
        proj = _norm_proj(h, w['norm1_g'], w['w_in'], min(2048, seqlen), 1024)
        q, k, v, bg = _dn_prep(proj, w['dn'], seqlen, tm)
        o_dn = _dn_scan(q, k, v, bg, bsz, seqlen)
        y_s5 = _s5_core(proj, w['s5'][seqlen], bsz, seqlen)
        r, rv, kk, lw, kd, b, bonus, gate_rw = _rw_prep(proj, w['rw'], seqlen, tm)
        y_rw = _rw_scan(r, rv, kk, lw, kd, b, bsz, seqlen)
        h = _merge(h, proj, o_dn, y_s5, y_rw, bonus, gate_rw, w['merge'], tm)
        h = _moe(h, w['norm2_g'], w['router_wt'], w['w1'], w['w3'], w['w2'],
                 lp['final_norm_g'] if li == depth - 1 else None)
    return h.reshape(bsz, seqlen, dm)
```

```python
import functools
import math

import jax
import jax.numpy as jnp
from jax import lax
from jax.experimental import pallas as pl
from jax.experimental.pallas import tpu as pltpu

F32 = jnp.float32
BF16 = jnp.bfloat16
HI = lax.Precision.HIGHEST

D_MODEL = 1024
NORM_EPS = 1e-6
CHUNK = 64
SCAN_LOCKSTEP = 16
DN_HEADS, DN_HEAD_DIM, DN_WIDTH, DN_CONV = 4, 128, 512, 5
S5_WIDTH, S5_GROUP, S5_GROUPS, S5_STATE = 512, 16, 32, 64
RW_WIDTH, RW_HEAD_DIM, RW_HEADS = 512, 64, 8
RW_DECAY_RANK, RW_AAA_RANK, RW_GATE_RANK = 64, 64, 128
RW_GN_EPS = 64e-5
RW_COLS = 3 * RW_WIDTH + 2 * RW_DECAY_RANK + 2 * RW_AAA_RANK + RW_GATE_RANK
N_EXPERTS, EXPERT_FF, CAPACITY_FACTOR = 16, 2048, 2
VMEM_LIMIT = 56 * 1024 * 1024

COL_GATE, COL_QKV, COL_Z, COL_S5, COL_BA, COL_RW, PROJ_COLS = 0, 3072, 4608, 5120, 5632, 6144, 8192
RW_PAD = 2048
HALO = 16


def _proj_source_columns():
    import numpy as np
    w = DN_WIDTH
    off_q, off_z, off_beta = 0, 3 * w, 4 * w
    off_alpha = off_beta + 2 * DN_HEADS
    off_s5 = off_alpha + 2 * DN_HEADS
    off_rw = off_s5 + S5_WIDTH
    off_gate = off_rw + RW_COLS
    src = np.full((PROJ_COLS,), -1, np.int32)
    src[COL_GATE:COL_GATE + 3 * D_MODEL] = off_gate + np.arange(3 * D_MODEL)
    src[COL_QKV:COL_QKV + 3 * w] = off_q + np.arange(3 * w)
    src[COL_Z:COL_Z + w] = off_z + np.arange(w)
    src[COL_S5:COL_S5 + S5_WIDTH] = off_s5 + np.arange(S5_WIDTH)
    lane = np.arange(16)
    src[COL_BA:COL_BA + 16] = np.where((lane & 4) != 0, off_alpha, off_beta) + ((lane >> 3) & 1) * DN_HEADS + (lane & 3)
    src[COL_RW:COL_RW + RW_COLS] = off_rw + np.arange(RW_COLS)
    return src


def _bdot(a, b):
    return jnp.dot(a.astype(BF16), b.astype(BF16), preferred_element_type=F32)


def _bdot_nt(a, b):
    return lax.dot_general(a.astype(BF16), b.astype(BF16), (((1,), (1,)), ((), ())),
                           preferred_element_type=F32)


def _bdot_tn(a, b):
    return lax.dot_general(a.astype(BF16), b.astype(BF16), (((0,), (0,)), ((), ())),
                           preferred_element_type=F32)


def _fdot(a, b):
    return jnp.dot(a, b, precision=HI, preferred_element_type=F32)


def _sigmoid(x):
    return 1.0 / (1.0 + jnp.exp(-x))


def _softplus(x):
    return jnp.maximum(x, 0.0) + jnp.log(1.0 + jnp.exp(-jnp.abs(x)))


def _silu(x):
    return x * _sigmoid(x)


def _cparams(n_axes):
    return pltpu.CompilerParams(dimension_semantics=("arbitrary",) * n_axes, vmem_limit_bytes=VMEM_LIMIT)


def _each(f, *lists):
    return [f(*xs) for xs in zip(*lists)]


def _unit_tri_inverse(ms, n):
    ri = lax.broadcasted_iota(jnp.int32, (n, n), 0)
    ci = lax.broadcasted_iota(jnp.int32, (n, n), 1)
    eye = (ri == ci).astype(F32)
    ts = _each(lambda m: eye - jnp.where((ri >> 1) == (ci >> 1), m, 0.0), ms)
    for lg in range(2, 7):
        couple = ((ri >> lg) == (ci >> lg)) & ((ri >> (lg - 1)) != (ci >> (lg - 1)))
        xs = _each(lambda m, t: _bdot(jnp.where(couple, m, 0.0), t), ms, ts)
        ts = _each(lambda t, x: t - _bdot(t, x), ts, xs)
    return ts


def _norm_proj_kernel(x_ref, g_ref, w_ref, o_ref, xn_ref):
    @pl.when(pl.program_id(1) == 0)
    def _():
        x = x_ref[...]
        y = x * lax.rsqrt(jnp.mean(x * x, axis=-1, keepdims=True) + NORM_EPS)
        xn_ref[...] = (y * g_ref[...]).astype(BF16)

    o_ref[...] = jnp.dot(xn_ref[...], w_ref[...], preferred_element_type=F32).astype(BF16)


def _norm_proj(x, g, w, tm, tn):
    n, d = x.shape
    cols = w.shape[1]
    return pl.pallas_call(
        _norm_proj_kernel,
        grid=(n // tm, cols // tn),
        in_specs=[pl.BlockSpec((tm, d), lambda i, j: (i, 0)),
                  pl.BlockSpec((1, d), lambda i, j: (0, 0)),
                  pl.BlockSpec((d, tn), lambda i, j: (0, j))],
        out_specs=pl.BlockSpec((tm, tn), lambda i, j: (i, j)),
        out_shape=jax.ShapeDtypeStruct((n, cols), BF16),
        scratch_shapes=[pltpu.VMEM((tm, d), BF16)],
        compiler_params=_cparams(2),
        name="norm_proj",
    )(x, g.reshape(1, d), w)


def _rw_prep_kernel(x_ref, xp_ref, xn_ref, mu_ref, w0_ref, w2_ref, a0_ref, a2_ref, g2_ref, kk_w_ref, ka_ref,
                    rk_ref, ones_ref, r_ref, v_ref, kk_ref, lw_ref, kd_ref, b_ref, bonus_ref, gate_ref,
                    *, tiles_per_seq):
    i = pl.program_id(0)
    tm = x_ref.shape[0]
    x = x_ref[...].astype(F32)
    first = (i % tiles_per_seq) == 0
    last = (i % tiles_per_seq) == tiles_per_seq - 1
    row = lax.broadcasted_iota(jnp.int32, x.shape, 0)
    prev_edge = jnp.where(first, 0.0, xp_ref[...].astype(F32)[HALO - 1:HALO, :])
    next_edge = jnp.where(last, 0.0, xn_ref[...].astype(F32)[0:1, :])
    prev = jnp.where(row == 0, prev_edge, pltpu.roll(x, 1, axis=0))
    nxt = jnp.where(row == tm - 1, next_edge, pltpu.roll(x, tm - 1, axis=0))
    f = x + (0.5 * (prev + nxt) - x) * mu_ref[...]
    w = RW_WIDTH
    r, k, v = f[:, 0:w], f[:, w:2 * w], f[:, 2 * w:3 * w]
    w_lo, a_lo, g_lo = f[:, 3 * w:3 * w + 128], f[:, 3 * w + 128:3 * w + 256], f[:, 3 * w + 256:3 * w + 384]
    ones_bd = ones_ref[...]
    kq = k * kk_w_ref[...]
    kk = kq * lax.rsqrt(_fdot(kq * kq, ones_bd) + NORM_EPS)
    gate_ref[...] = _bdot(_sigmoid(g_lo), g2_ref[...]).astype(BF16)
    w_log = -_softplus(-(w0_ref[...] + _bdot(jnp.tanh(w_lo), w2_ref[...]))) - 0.5
    lw_ref[...] = -jnp.exp(w_log)
    a_lr = _sigmoid(a0_ref[...] + _bdot(a_lo, a2_ref[...]))
    ka = ka_ref[...]
    kd0 = k * (1.0 + (a_lr[:, 0:w] - 1.0) * ka)
    kd1 = k * (1.0 + (a_lr[:, w:2 * w] - 1.0) * ka)
    kd_ref[:, 0:w] = kd0.astype(BF16)
    kd_ref[:, w:2 * w] = kd1.astype(BF16)
    b_ref[:, 0:w] = (kk * a_lr[:, 0:w]).astype(BF16)
    b_ref[:, w:2 * w] = (kk * a_lr[:, w:2 * w]).astype(BF16)
    bonus_ref[...] = (_fdot(r * (kd0 + kd1) * rk_ref[...], ones_bd) * v).astype(BF16)
    r_ref[...] = r.astype(BF16)
    v_ref[...] = v.astype(BF16)
    kk_ref[...] = kk.astype(BF16)


def _rw_prep(proj, p, seqlen, tm):
    n = proj.shape[0]
    w = RW_WIDTH
    tps = seqlen // tm
    nhalo = n // HALO
    full = lambda a: pl.BlockSpec(a.shape, lambda i: (0,) * a.ndim)
    consts = [p['mu'], p['w0'], p['w2'], p['a0'], p['a2'], p['g2'], p['k_k'], p['k_a'], p['r_k'], p['ones']]
    out_w = [w, w, w, 2 * w, 2 * w, 2 * w, w, w]
    return pl.pallas_call(
        functools.partial(_rw_prep_kernel, tiles_per_seq=tps),
        grid=(n // tm,),
        in_specs=[pl.BlockSpec((tm, RW_PAD), lambda i: (i, COL_RW // RW_PAD)),
                  pl.BlockSpec((HALO, RW_PAD), lambda i: (jnp.maximum(i * (tm // HALO) - 1, 0), COL_RW // RW_PAD)),
                  pl.BlockSpec((HALO, RW_PAD), lambda i: (jnp.minimum((i + 1) * (tm // HALO), nhalo - 1),
                                                          COL_RW // RW_PAD))]
                 + [full(a) for a in consts],
        out_specs=[pl.BlockSpec((tm, c), lambda i: (i, 0)) for c in out_w],
        out_shape=[jax.ShapeDtypeStruct((n, c), F32 if j == 3 else BF16) for j, c in enumerate(out_w)],
        compiler_params=_cparams(1),
        name="rw_prep",
    )(proj, proj, proj, *consts)


def _scan_masks(n, fwd):
    c = CHUNK
    ri = lax.broadcasted_iota(jnp.int32, (n, n), 0)
    ci = lax.broadcasted_iota(jnp.int32, (n, n), 1)
    ti, tj = ri & (c - 1), ci & (c - 1)
    same_head = (ri >> 6) == (ci >> 6)
    before = (tj < ti) if fwd else (tj > ti)
    strict = same_head & before
    incl = same_head & (before | (ti == tj))
    i64 = lax.broadcasted_iota(jnp.int32, (c, c), 0)
    j64 = lax.broadcasted_iota(jnp.int32, (c, c), 1)
    cum_mask = ((j64 <= i64) if fwd else (j64 >= i64)).astype(F32)
    return strict, incl, cum_mask


def _rw_chunks(fwd, masks, r, v, kk, lw, kd, bb, ht_ref, slots):
    c, pw = CHUNK, 128
    strict, incl = [m[0] for m in masks], [m[1] for m in masks]
    head0 = lax.broadcasted_iota(jnp.int32, (c, pw), 1) < RW_HEAD_DIM
    stack = lambda x: jnp.concatenate([jnp.where(head0, x, 0.0), jnp.where(head0, 0.0, x)], axis=0)
    cat0 = lambda a, b: jnp.concatenate([a, b], axis=0)

    cum = _each(lambda m, x: _fdot(m[2], x), masks, lw)
    tot = _each(lambda x, f: x[c - 1:c, :] if f else x[0:1, :], cum, fwd)
    p_inv = _each(lambda x: jnp.exp(-x), cum)
    p_end_over = _each(lambda t, x: jnp.exp(t - x), tot, cum)
    rd_s = _each(lambda x, cm: stack(x * jnp.exp(cm)), r, cum)
    kp_s = _each(lambda x, cm, l: stack(x * jnp.exp(cm - l)), kk, cum, lw)
    kb_inv = _each(lambda k, b, pi: cat0(stack(k * pi), stack(b * pi)), kd, bb, p_inv)
    kb_end = _each(lambda k, b, pe: cat0(stack(k * pe), stack(b * pe)), kd, bb, p_end_over)
    v_s = _each(stack, v)
    a_k = _each(_bdot_nt, kp_s, kb_inv)
    a_r = _each(_bdot_nt, rd_s, kb_inv)
    t_inv = _unit_tri_inverse(_each(lambda a, st: jnp.where(st, a[:, pw:], 0.0), a_k, strict), pw)
    ht = [ht_ref[s] for s in slots]
    carry = _each(lambda kp, rd, h: _bdot_nt(cat0(kp, rd), h), kp_s, rd_s, ht)
    av = _each(lambda a, x, st: _bdot(jnp.where(st, a[:, :pw], 0.0), x), a_k, v_s, strict)
    u_s = _each(lambda t, cr, x: _bdot(t, cr[:pw] + x), t_inv, carry, av)
    y_s = _each(lambda a, cr, x, u, inc: cr[pw:] + _bdot(
        jnp.concatenate([jnp.where(inc, a[:, :pw], 0.0), -jnp.where(inc, a[:, pw:], 0.0)], axis=1), cat0(x, u)),
        a_r, carry, v_s, u_s, incl)
    upd = _each(lambda x, u, ke: _bdot_tn(cat0(x, -u), ke), v_s, u_s, kb_end)
    for s, h, t, up in zip(slots, ht, tot, upd):
        ht_ref[s] = h * jnp.exp(t) + up
    return _each(lambda y: y[:c] + y[c:], y_s)


def _rw_scan_kernel(*refs, sb):
    ins, (yf_ref, yb_ref, ht_ref) = refs[:12], refs[12:]

    @pl.when(pl.program_id(1) == 0)
    def _():
        ht_ref[...] = jnp.zeros_like(ht_ref)

    npair = RW_WIDTH // 128
    y_refs = (yf_ref, yb_ref)
    both = [_scan_masks(128, True), _scan_masks(128, False)]
    items = [(d, s, p) for d in range(2) for s in range(sb) for p in range(npair)]
    for g in range(0, len(items), SCAN_LOCKSTEP):
        grp = items[g:g + SCAN_LOCKSTEP]
        tiles = [[ins[6 * d + j][s, :, 128 * p:128 * (p + 1)].astype(F32) for d, s, p in grp] for j in range(6)]
        ys = _rw_chunks([d == 0 for d, s, p in grp], [both[d] for d, s, p in grp], *tiles, ht_ref,
                        [(d * sb + s) * npair + p for d, s, p in grp])
        for (d, s, p), y in zip(grp, ys):
            y_refs[d][s, :, 128 * p:128 * (p + 1)] = y.astype(y_refs[d].dtype)


def _scan_batch(bsz):
    return next(s for s in (4, 2, 1) if bsz % s == 0)


def _rw_scan(r, v, kk, lw, kd, b, bsz, seqlen):
    n, w = r.shape
    nc = seqlen // CHUNK
    sb = _scan_batch(bsz)
    r, v, kk = (a.reshape(bsz, seqlen, w) for a in (r, v, kk))
    lw, kd, b = (a.reshape(bsz, seqlen, 2 * w) for a in (lw, kd, b))
    spec = lambda d, col: pl.BlockSpec((sb, CHUNK, w), lambda bi, c: (bi, c + d * (nc - 1 - 2 * c), col))
    yf, yb = pl.pallas_call(
        functools.partial(_rw_scan_kernel, sb=sb),
        grid=(bsz // sb, nc),
        in_specs=[spec(0, 0)] * 6 + [spec(1, 0)] * 3 + [spec(1, 1)] * 3,
        out_specs=[spec(0, 0), spec(1, 0)],
        out_shape=[jax.ShapeDtypeStruct((bsz, seqlen, w), BF16)] * 2,
        scratch_shapes=[pltpu.VMEM((2 * sb * (w // 128), 128, 128), F32)],
        compiler_params=_cparams(2),
        name="rw_scan",
    )(r, v, kk, lw, kd, b, r, v, kk, lw, kd, b)
    return yf.reshape(n, w), yb.reshape(n, w)


def _rw_params(p, layer):
    w = RW_WIDTH
    pad = lambda a: jnp.pad(a, ((0, 0), (0, RW_PAD - a.shape[1])))
    blockdiag2 = lambda m: jnp.concatenate(
        [jnp.concatenate([m[0], jnp.zeros_like(m[0])], axis=1),
         jnp.concatenate([jnp.zeros_like(m[1]), m[1]], axis=1)], axis=0)
    hid = jnp.arange(w) // RW_HEAD_DIM
    return {
        'mu': pad(p['rw_mu'][layer][None, :]),
        'w0': p['rw_w0'][layer].reshape(1, 2 * w),
        'w2': blockdiag2(p['rw_w2'][layer]).astype(BF16),
        'a0': p['rw_a0'][layer].reshape(1, 2 * w),
        'a2': blockdiag2(p['rw_a2'][layer]).astype(BF16),
        'g2': p['rw_g2'][layer].astype(BF16),
        'k_k': p['rw_k_k'][layer][None, :],
        'k_a': p['rw_k_a'][layer][None, :],
        'r_k': p['rw_r_k'][layer].reshape(1, w),
        'ones': (hid[:, None] == hid[None, :]).astype(F32),
    }


def _dn_prep_kernel(x_ref, xp_ref, xn_ref, ba_ref, cw_ref, alog_ref, dtb_ref, ones_ref,
                    q_ref, k_ref, v_ref, bg_ref, *, tiles_per_seq):
    i = pl.program_id(0)
    tm = x_ref.shape[0]
    x = x_ref[...].astype(F32)
    first = (i % tiles_per_seq) == 0
    last = (i % tiles_per_seq) == tiles_per_seq - 1
    row = lax.broadcasted_iota(jnp.int32, x.shape, 0)
    xp = jnp.where(first, 0.0, xp_ref[...].astype(F32))
    xn = jnp.where(last, 0.0, xn_ref[...].astype(F32))
    pad = (DN_CONV - 1) // 2
    acc = x * cw_ref[pad:pad + 1, :]
    for s in range(1, pad + 1):
        back = pltpu.roll(x, s, axis=0)
        fore = pltpu.roll(x, tm - s, axis=0)
        for t in range(s):
            back = jnp.where(row == t, xp[HALO - s + t:HALO + 1 - s + t, :], back)
            fore = jnp.where(row == tm - 1 - t, xn[s - 1 - t:s - t, :], fore)
        acc = acc + back * cw_ref[pad - s:pad - s + 1, :] + fore * cw_ref[pad + s:pad + s + 1, :]
    y = _silu(acc)
    w = DN_WIDTH
    ones_bd = ones_ref[...]
    q, k = y[:, 0:w], y[:, w:2 * w]
    q_ref[...] = (q * (lax.rsqrt(_fdot(q * q, ones_bd) + NORM_EPS) * (DN_HEAD_DIM ** -0.5))).astype(BF16)
    k_ref[...] = (k * lax.rsqrt(_fdot(k * k, ones_bd) + NORM_EPS)).astype(BF16)
    v_ref[...] = y[:, 2 * w:3 * w].astype(BF16)
    ba = ba_ref[...].astype(F32)
    lane = lax.broadcasted_iota(jnp.int32, ba.shape, 1)
    g = -jnp.exp(alog_ref[...]) * _softplus(ba + dtb_ref[...])
    bg_ref[...] = jnp.where((lane & 4) == 0, _sigmoid(ba), g)


def _dn_prep(proj, p, seqlen, tm):
    n = proj.shape[0]
    w = DN_WIDTH
    tps = seqlen // tm
    nhalo = n // HALO
    full = lambda a: pl.BlockSpec(a.shape, lambda i: (0,) * a.ndim)
    consts = [p['conv_w'], p['a_log'], p['dt_bias'], p['ones']]
    cb = COL_QKV // (3 * w)
    return pl.pallas_call(
        functools.partial(_dn_prep_kernel, tiles_per_seq=tps),
        grid=(n // tm,),
        in_specs=[pl.BlockSpec((tm, 3 * w), lambda i: (i, cb)),
                  pl.BlockSpec((HALO, 3 * w), lambda i: (jnp.maximum(i * (tm // HALO) - 1, 0), cb)),
                  pl.BlockSpec((HALO, 3 * w), lambda i: (jnp.minimum((i + 1) * (tm // HALO), nhalo - 1), cb)),
                  pl.BlockSpec((tm, 128), lambda i: (i, COL_BA // 128))]
                 + [full(a) for a in consts],
        out_specs=[pl.BlockSpec((tm, c), lambda i: (i, 0)) for c in (w, w, w, 128)],
        out_shape=[jax.ShapeDtypeStruct((n, c), F32 if c == 128 else BF16) for c in (w, w, w, 128)],
        compiler_params=_cparams(1),
        name="dn_prep",
    )(proj, proj, proj, proj, *consts)


def _dn_chunks(fwd, masks, q, k, v, bg, s_ref, slots):
    c, hd, nh = CHUNK, DN_HEAD_DIM, DN_HEADS
    n = nh * c
    heads = range(nh)
    strict, incl = [m[0] for m in masks], [m[1] for m in masks]
    lane0 = [0 if f else 8 for f in fwd]
    gcum_all = _each(lambda m, x: _fdot(m[2], x), masks, bg)
    bgd = _each(lambda x, l: x[:, l:l + 8], bg, lane0)
    gcum = _each(lambda x, l: x[:, l:l + 8], gcum_all, lane0)
    gtot = _each(lambda x, f: x[c - 1:c, :] if f else x[0:1, :], gcum, fwd)
    beta_s = _each(lambda x: jnp.concatenate([x[:, h:h + 1] for h in heads], axis=0), bgd)
    gc_s = _each(lambda x: jnp.concatenate([x[:, nh + h:nh + h + 1] for h in heads], axis=0), gcum)
    gend_s = _each(lambda x: jnp.concatenate(
        [jnp.broadcast_to(x[:, nh + h:nh + h + 1], (c, 1)) for h in heads], axis=0), gtot)
    stack = lambda x: jnp.concatenate([x[:, hd * h:hd * (h + 1)] for h in heads], axis=0)
    q_s, k_s, v_s = _each(stack, q), _each(stack, k), _each(stack, v)

    def decay_of(g, inc):
        rows = jnp.broadcast_to(g, (n, n))
        return jnp.exp(jnp.where(inc, rows - rows.T, -1e30))

    decay = _each(decay_of, gc_s, incl)
    kb_s = _each(lambda x, b: x * b, k_s, beta_s)
    qk = _each(lambda kb, x, kk: _bdot_nt(jnp.concatenate([kb, x], axis=0), kk), kb_s, q_s, k_s)
    t_inv = _unit_tri_inverse(_each(lambda a, dc, st: jnp.where(st, a[:n] * dc, 0.0), qk, decay, strict), n)
    a_qk = _each(lambda a, dc, inc: jnp.where(inc, a[n:] * dc, 0.0), qk, decay, incl)
    e_gc = _each(jnp.exp, gc_s)
    uw = _each(lambda t, x, b, kb, e: _bdot(t, jnp.concatenate([x * b, kb * e], axis=1)),
               t_inv, v_s, beta_s, kb_s, e_gc)
    q_dec = _each(lambda x, e: x * e, q_s, e_gc)
    k_dec = _each(lambda x, ge, g: x * jnp.exp(ge - g), k_s, gend_s, gc_s)
    e_end = _each(jnp.exp, gtot)
    rows = [slice(c * h, c * (h + 1)) for h in heads]
    states = [[s_ref[sl + h] for h in heads] for sl in slots]
    carry = _each(lambda x, qd, st: [_bdot(jnp.concatenate([x[rows[h], hd:], qd[rows[h]]], axis=0), st[h])
                                     for h in heads], uw, q_dec, states)
    v_new = _each(lambda x, cr: x[:, :hd] - jnp.concatenate([cr[h][:c] for h in heads], axis=0), uw, carry)
    o_s = _each(lambda cr, a, x: jnp.concatenate([cr[h][c:] for h in heads], axis=0) + _bdot(a, x),
                carry, a_qk, v_new)
    upd = _each(lambda kd, x: [_bdot_tn(kd[rows[h]], x[rows[h]]) for h in heads], k_dec, v_new)
    for sl, st, e, up in zip(slots, states, e_end, upd):
        for h in heads:
            s_ref[sl + h] = st[h] * e[:, nh + h:nh + h + 1] + up[h]
    return _each(lambda x: jnp.concatenate([x[rows[h]] for h in heads], axis=1), o_s)


def _dn_scan_kernel(*refs, sb):
    ins, outs, s_ref = refs[:8], refs[8:10], refs[10]

    @pl.when(pl.program_id(1) == 0)
    def _():
        s_ref[...] = jnp.zeros_like(s_ref)

    items = [(d, s) for d in range(2) for s in range(sb)]
    both = [_scan_masks(DN_HEADS * CHUNK, True), _scan_masks(DN_HEADS * CHUNK, False)]
    tiles = [[ins[4 * d + j][s].astype(F32) for d, s in items] for j in range(4)]
    os_ = _dn_chunks([d == 0 for d, s in items], [both[d] for d, s in items], *tiles, s_ref,
                     [(d * sb + s) * DN_HEADS for d, s in items])
    for (d, s), o in zip(items, os_):
        outs[d][s] = o.astype(outs[d].dtype)


def _dn_scan(q, k, v, bg, bsz, seqlen):
    n, w = q.shape
    nc = seqlen // CHUNK
    sb = _scan_batch(bsz)
    q, k, v = (a.reshape(bsz, seqlen, w) for a in (q, k, v))
    bg = bg.reshape(bsz, seqlen, 128)
    spec = lambda d, cols: pl.BlockSpec((sb, CHUNK, cols), lambda bi, c: (bi, c + d * (nc - 1 - 2 * c), 0))
    dir_specs = lambda d: [spec(d, w)] * 3 + [spec(d, 128)]
    of, ob = pl.pallas_call(
        functools.partial(_dn_scan_kernel, sb=sb),
        grid=(bsz // sb, nc),
        in_specs=dir_specs(0) + dir_specs(1),
        out_specs=[spec(0, w), spec(1, w)],
        out_shape=[jax.ShapeDtypeStruct((bsz, seqlen, w), BF16)] * 2,
        scratch_shapes=[pltpu.VMEM((2 * sb * DN_HEADS, DN_HEAD_DIM, DN_HEAD_DIM), F32)],
        compiler_params=_cparams(2),
        name="dn_scan",
    )(q, k, v, bg, q, k, v, bg)
    return of.reshape(n, w), ob.reshape(n, w)


def _dn_params(p, layer):
    lane = jnp.arange(128)
    dirn, hh, is_alpha = (lane >> 3) & 1, lane & 3, ((lane & 4) != 0) & (lane < 16)
    a_log = jnp.where(is_alpha, p['dn_a_log'][layer][dirn, hh], 0.0)
    dt_bias = jnp.where(is_alpha, p['dn_dt_bias'][layer][dirn, hh], 0.0)
    hid = jnp.arange(DN_WIDTH) // DN_HEAD_DIM
    return {
        'conv_w': jnp.pad(p['dn_conv_w'][layer], ((0, 8 - DN_CONV), (0, 0))),
        'a_log': a_log[None, :].astype(F32),
        'dt_bias': dt_bias[None, :].astype(F32),
        'ones': (hid[:, None] == hid[None, :]).astype(F32),
    }


S5_CHUNK = 128
S5_RELAYOUT_TOKENS = 512
S5_OUT_SPLIT = 4


def _s5_to_channel_major_kernel(x_ref, a_ref):
    for j in range(x_ref.shape[0] // S5_CHUNK):
        xt = x_ref[S5_CHUNK * j:S5_CHUNK * (j + 1), :].astype(F32).T
        a_ref[:, S5_GROUP * j:S5_GROUP * (j + 1), :] = xt.reshape(S5_GROUPS, S5_GROUP, S5_CHUNK)


def _s5_to_token_major_kernel(y_ref, o_ref):
    for j in range(o_ref.shape[0] // S5_CHUNK):
        yt = y_ref[:, S5_GROUP * j:S5_GROUP * (j + 1), :].reshape(S5_WIDTH, S5_CHUNK)
        o_ref[S5_CHUNK * j:S5_CHUNK * (j + 1), :] = yt.T.astype(o_ref.dtype)


def _s5_kernel(a_ref, ws_ref, w_ref, wo_ref, ca_ref, cb_ref, y_ref, *, levels, nblk):
    nb = a_ref.shape[0] // S5_GROUP
    a = [a_ref[pl.ds(e, nb, stride=S5_GROUP), :].astype(BF16) for e in range(S5_GROUP)]
    x = sum(jnp.dot(a[e], ws_ref[e], preferred_element_type=F32) for e in range(S5_GROUP))
    row = lax.broadcasted_iota(jnp.int32, (nb, 128), 0) & (nblk - 1)
    ca, cb = ca_ref[...], cb_ref[...]

    def cmul(k, half, h):
        a_k = ca[k:k + 1, 128 * half:128 * (half + 1)]
        b_k = cb[k:k + 1, 128 * half:128 * (half + 1)]
        return a_k * h + b_k * pltpu.roll(h, S5_STATE, axis=1)

    hf, hb = x[:, 0:128], x[:, 128:256]
    for k in range(levels):
        s = 1 << k
        hf = hf + jnp.where(row >= s, cmul(k, 0, pltpu.roll(hf, s, axis=0)), 0.0)
        hb = hb + jnp.where(row < nblk - s, cmul(k, 1, pltpu.roll(hb, nb - s, axis=0)), 0.0)
    hf = jnp.where(row >= 1, pltpu.roll(hf, 1, axis=0), 0.0)
    hb = jnp.where(row < nblk - 1, pltpu.roll(hb, nb - 1, axis=0), 0.0)
    h = jnp.concatenate([hf, hb], axis=1).astype(BF16)
    cols = S5_OUT_SPLIT * S5_CHUNK
    for q in range(S5_GROUP // S5_OUT_SPLIT):
        sl = slice(cols * q, cols * (q + 1))
        y = jnp.dot(h, wo_ref[:, sl], preferred_element_type=F32)
        for e in range(S5_GROUP):
            y = y + jnp.dot(a[e], w_ref[e, :, sl], preferred_element_type=F32)
        for j in range(S5_OUT_SPLIT):
            y_ref[pl.ds(S5_OUT_SPLIT * q + j, nb, stride=S5_GROUP), :] = y[:, S5_CHUNK * j:S5_CHUNK * (j + 1)]


def _s5_core(proj, sp, bsz, seqlen):
    n = proj.shape[0]
    nblk = seqlen // S5_CHUNK
    nb = n // S5_CHUNK
    tt = min(S5_RELAYOUT_TOKENS, seqlen)
    rows = tt // S5_CHUNK * S5_GROUP
    grouped = pl.BlockSpec((S5_GROUPS, rows, S5_CHUNK), lambda i: (0, i, 0))
    a = pl.pallas_call(
        _s5_to_channel_major_kernel,
        grid=(n // tt,),
        in_specs=[pl.BlockSpec((tt, S5_WIDTH), lambda i: (i, COL_S5 // S5_WIDTH))],
        out_specs=grouped,
        out_shape=jax.ShapeDtypeStruct((S5_GROUPS, nb * S5_GROUP, S5_CHUNK), F32),
        compiler_params=_cparams(1),
        name="s5_to_channel_major",
    )(proj)
    wspec = lambda arr: pl.BlockSpec((None,) + arr.shape[1:], lambda g: (g,) + (0,) * (arr.ndim - 1))
    per_group = pl.BlockSpec((None, nb * S5_GROUP, S5_CHUNK), lambda g: (g, 0, 0))
    weights = [sp['ws'], sp['w'], sp['wo'], sp['ca'], sp['cb']]
    yg = pl.pallas_call(
        functools.partial(_s5_kernel, levels=sp['ca'].shape[1], nblk=nblk),
        grid=(S5_GROUPS,),
        in_specs=[per_group] + [wspec(arr) for arr in weights],
        out_specs=per_group,
        out_shape=jax.ShapeDtypeStruct((S5_GROUPS, nb * S5_GROUP, S5_CHUNK), F32),
        compiler_params=_cparams(1),
        name="s5_core",
    )(a, *weights)
    return pl.pallas_call(
        _s5_to_token_major_kernel,
        grid=(n // tt,),
        in_specs=[grouped],
        out_specs=pl.BlockSpec((tt, S5_WIDTH), lambda i: (i, 0)),
        out_shape=jax.ShapeDtypeStruct((n, S5_WIDTH), BF16),
        compiler_params=_cparams(1),
        name="s5_to_token_major",
    )(yg)


def _s5_params(p, layer, seqlen):
    cs, gs, ps = S5_CHUNK, S5_GROUP, S5_STATE
    cexp = lambda zr, zi: (jnp.exp(zr) * jnp.cos(zi), jnp.exp(zr) * jnp.sin(zi))
    cmul = lambda a, b: (a[0] * b[0] - a[1] * b[1], a[0] * b[1] + a[1] * b[0])
    lam_re, lam_im = p['s5_lam_re'][layer].astype(F32), p['s5_lam_im'][layer].astype(F32)
    step = jnp.exp(p['s5_log_step'][layer].astype(F32))[..., None]
    zr, zi = lam_re * step, lam_im * step
    lbar_re, lbar_im = cexp(zr, zi)
    den = lam_re * lam_re + lam_im * lam_im
    num_re = lbar_re - 1.0
    f = ((num_re * lam_re + lbar_im * lam_im) / den, (lbar_im * lam_re - num_re * lam_im) / den)
    bbar = cmul((f[0][..., None], f[1][..., None]),
                (p['s5_b_re'][layer].astype(F32), p['s5_b_im'][layer].astype(F32)))
    cc = (p['s5_c_re'][layer].astype(F32), p['s5_c_im'][layer].astype(F32))
    tau = jnp.arange(cs + 1, dtype=F32)
    pw = cexp(zr[..., None] * tau, zi[..., None] * tau)
    m = cmul((pw[0][..., None], pw[1][..., None]), (bbar[0][..., None, :], bbar[1][..., None, :]))
    kern = (jnp.einsum('dgcp,dgpte->dgtce', cc[0], m[0], precision=HI)
            - jnp.einsum('dgcp,dgpte->dgtce', cc[1], m[1], precision=HI))
    s_i, t_i = jnp.arange(cs)[:, None], jnp.arange(cs)[None, :]
    kf = jnp.where((t_i >= s_i)[None, :, :, None, None], kern[0][:, jnp.clip(t_i - s_i, 0, cs)], 0.0)
    kb = jnp.where((s_i >= t_i)[None, :, :, None, None], kern[1][:, jnp.clip(s_i - t_i, 0, cs)], 0.0)
    w = (kf + kb).astype(BF16).transpose(0, 4, 1, 3, 2).reshape(S5_GROUPS, gs, cs, gs * cs)
    sidx = jnp.arange(cs)
    sel = lambda d, idx: (pw[0][d][..., idx][..., None], pw[1][d][..., idx][..., None])
    inj = lambda d: (bbar[0][d][:, :, None, :], bbar[1][d][:, :, None, :])
    ws_f = cmul(sel(0, cs - 1 - sidx), inj(0))
    ws_b = cmul(sel(1, sidx), inj(1))
    to_cols = lambda m_: jnp.concatenate([m_[0], m_[1]], axis=1).transpose(0, 3, 2, 1)
    ws = jnp.concatenate([to_cols(ws_f), to_cols(ws_b)], axis=3)
    out = lambda d: (cc[0][d].transpose(0, 2, 1)[:, :, None, :], cc[1][d].transpose(0, 2, 1)[:, :, None, :])
    wo_f = cmul(out(0), sel(0, 1 + sidx))
    wo_b = cmul(out(1), sel(1, cs - sidx))
    to_rows = lambda m_: jnp.concatenate([m_[0], -m_[1]], axis=1).transpose(0, 1, 3, 2).reshape(
        S5_GROUPS, 2 * ps, gs * cs)
    wo = jnp.concatenate([to_rows(wo_f), to_rows(wo_b)], axis=1)
    levels = max(1, (seqlen // cs - 1).bit_length())
    span = cs * (2.0 ** jnp.arange(levels, dtype=F32))
    lk = cexp(zr[..., None] * span, zi[..., None] * span)
    re, im = lk[0].transpose(0, 1, 3, 2), lk[1].transpose(0, 1, 3, 2)
    ca = jnp.concatenate([re[0], re[0], re[1], re[1]], axis=-1)
    cb = jnp.concatenate([-im[0], im[0], -im[1], im[1]], axis=-1)
    return {'ws': ws.astype(BF16), 'w': w, 'wo': wo.astype(BF16), 'ca': ca.astype(F32), 'cb': cb.astype(F32)}


def _group_sum(x, ones_bf16):
    hi = x.astype(BF16)
    lo = (x - hi.astype(F32)).astype(BF16)
    return (jnp.dot(hi, ones_bf16, preferred_element_type=F32) + jnp.dot(lo, ones_bf16, preferred_element_type=F32))


def _gelu_tanh(x):
    return 0.5 * x * (1.0 + jnp.tanh(math.sqrt(2.0 / math.pi) * (x + 0.044715 * (x * x * x))))


def _merge_kernel(h_ref, gate_ref, z_ref, u_ref, odnf_ref, odnb_ref, ys5_ref, yrwf_ref, yrwb_ref, bonus_ref,
                  grw_ref, ones128_ref, ones64_ref, dng_ref, s5d_ref, glub_ref, lng_ref, lnb_ref,
                  gluw_ref, dnp_ref, s5p_ref, rwp_ref, wout_ref, o_ref):
    f32 = lambda ref: ref[...].astype(F32)
    o = f32(odnf_ref) + f32(odnb_ref)
    ms = _group_sum(o * o, ones128_ref[...]) * (1.0 / DN_HEAD_DIM)
    o_a = o * lax.rsqrt(ms + NORM_EPS) * dng_ref[...] * _silu(f32(z_ref))
    y = _gelu_tanh(f32(u_ref) * s5d_ref[...] + f32(ys5_ref))
    o_b = y * _sigmoid(_bdot(y, gluw_ref[...]) + glub_ref[...])
    yr = f32(yrwf_ref) + f32(yrwb_ref)
    ones64 = ones64_ref[...]
    cen = yr - _group_sum(yr, ones64) * (1.0 / RW_HEAD_DIM)
    var = _group_sum(cen * cen, ones64) * (1.0 / RW_HEAD_DIM)
    o_c = (cen * lax.rsqrt(var + RW_GN_EPS) * lng_ref[...] + lnb_ref[...] + f32(bonus_ref)) * f32(grw_ref)
    dm = D_MODEL
    gate = lambda j: _sigmoid(gate_ref[:, j * dm:(j + 1) * dm].astype(F32))
    merged = (gate(0) * _bdot(o_a, dnp_ref[...]) + gate(1) * _bdot(o_b, s5p_ref[...])
              + gate(2) * _bdot(o_c, rwp_ref[...]))
    o_ref[...] = h_ref[...] + _bdot(merged, wout_ref[...])


def _merge(h, proj, o_dn, y_s5, y_rw, bonus, gate_rw, mp, tm):
    n, dm = h.shape
    w = DN_WIDTH
    row = lambda cols, cb=0: pl.BlockSpec((tm, cols), lambda i: (i, cb))
    full = lambda a: pl.BlockSpec(a.shape, lambda i: (0,) * a.ndim)
    consts = [mp['ones128'], mp['ones64'], mp['dn_norm_g'], mp['s5_d'], mp['glu_b'], mp['ln_g'], mp['ln_b'],
              mp['glu_w'], mp['dn_proj'], mp['s5_proj'], mp['rw_proj'], mp['w_out']]
    return pl.pallas_call(
        _merge_kernel,
        grid=(n // tm,),
        in_specs=[row(dm), row(3 * dm, COL_GATE // (3 * dm)), row(w, COL_Z // w), row(w, COL_S5 // w),
                  row(w), row(w), row(w), row(w), row(w), row(w), row(w)] + [full(a) for a in consts],
        out_specs=row(dm),
        out_shape=jax.ShapeDtypeStruct((n, dm), F32),
        compiler_params=_cparams(1),
        name="merge",
    )(h, proj, proj, proj, o_dn[0], o_dn[1], y_s5, y_rw[0], y_rw[1], bonus, gate_rw, *consts)


def _merge_params(p, layer):
    hid128 = jnp.arange(DN_WIDTH) // DN_HEAD_DIM
    hid64 = jnp.arange(RW_WIDTH) // RW_HEAD_DIM
    r1 = lambda a: a.reshape(1, -1).astype(F32)
    return {
        'ones128': (hid128[:, None] == hid128[None, :]).astype(BF16),
        'ones64': (hid64[:, None] == hid64[None, :]).astype(BF16),
        'dn_norm_g': r1(jnp.tile(p['dn_norm_g'][layer], DN_HEADS)),
        's5_d': r1(p['s5_d'][layer]), 'glu_b': r1(p['s5_glu_b'][layer]),
        'ln_g': r1(p['rw_ln_g'][layer]), 'ln_b': r1(p['rw_ln_b'][layer]),
        'glu_w': p['s5_glu_w'][layer].astype(BF16), 'dn_proj': p['dn_proj'][layer].astype(BF16),
        's5_proj': p['s5_proj'][layer].astype(BF16), 'rw_proj': p['rw_proj'][layer].astype(BF16),
        'w_out': p['w_out'][layer].astype(BF16),
    }


MOE_TILE = 128
MOE_WIN = MOE_TILE + 16
MOE_GATHER_TILES = 16
MOE_ROWS = 512


def _slot_rows(cap):
    tr = min(MOE_ROWS, cap)
    return -(-(cap + MOE_WIN) // tr) * tr


def _router_kernel(h_ref, g_ref, rw_ref, hn_ref, aff_ref):
    x = h_ref[...]
    hn = (x * lax.rsqrt(jnp.mean(x * x, axis=-1, keepdims=True) + NORM_EPS) * g_ref[...]).astype(BF16)
    hn_ref[...] = hn
    logits = lax.dot_general(rw_ref[...], hn, (((1,), (1,)), ((), ())), preferred_element_type=F32)
    ex = jnp.exp(logits - jnp.max(logits, axis=0, keepdims=True))
    aff_ref[...] = ex / jnp.sum(ex, axis=0, keepdims=True)


def _router(h, g, router_wt, tm):
    n, dm = h.shape
    return pl.pallas_call(
        _router_kernel,
        grid=(n // tm,),
        in_specs=[pl.BlockSpec((tm, dm), lambda i: (i, 0)), pl.BlockSpec((1, dm), lambda i: (0, 0)),
                  pl.BlockSpec((N_EXPERTS, dm), lambda i: (0, 0))],
        out_specs=[pl.BlockSpec((tm, dm), lambda i: (i, 0)), pl.BlockSpec((N_EXPERTS, tm), lambda i: (0, i))],
        out_shape=[jax.ShapeDtypeStruct((n, dm), BF16), jax.ShapeDtypeStruct((N_EXPERTS, n), F32)],
        compiler_params=_cparams(1),
        name="router",
    )(h, g.reshape(1, dm), router_wt)


def _threshold_kernel(aff_ref, thr_ref, *, cap):
    bits = pltpu.bitcast(aff_ref[...], jnp.int32)

    def body(i, t):
        cand = t | (jnp.int32(1) << (30 - i))
        cnt = jnp.sum((bits >= cand).astype(F32), axis=1, keepdims=True)
        return jnp.where(cnt >= cap, cand, t)

    t = lax.fori_loop(0, 31, body, jnp.zeros((N_EXPERTS, 1), jnp.int32))
    thr_ref[...] = jnp.broadcast_to(t, thr_ref.shape)


def _threshold(aff_t, cap):
    e, n = aff_t.shape
    return pl.pallas_call(
        functools.partial(_threshold_kernel, cap=cap),
        grid=(1,),
        in_specs=[pl.BlockSpec((e, n), lambda i: (0, 0))],
        out_specs=pl.BlockSpec((e, 128), lambda i: (0, 0)),
        out_shape=jax.ShapeDtypeStruct((e, 128), jnp.int32),
        compiler_params=_cparams(1),
        name="moe_threshold",
    )(aff_t)


def _slots_kernel(thr_ref, aff_ref, slot_ref, posx_ref, *, cap):
    e = pl.program_id(0)
    thr = thr_ref[e]
    bits = pltpu.bitcast(aff_ref[...], jnp.int32)
    nb = bits.shape[0]
    li = lax.broadcasted_iota(jnp.int32, (128, 128), 0)
    lj = lax.broadcasted_iota(jnp.int32, (128, 128), 1)
    upper = (li <= lj).astype(BF16)
    bi = lax.broadcasted_iota(jnp.int32, (nb, nb), 0)
    bj = lax.broadcasted_iota(jnp.int32, (nb, nb), 1)
    lower_strict = (bj < bi).astype(BF16)

    def exclusive_count(m):
        incl = jnp.dot(m.astype(BF16), upper, preferred_element_type=F32)
        tot = jnp.broadcast_to(incl[:, 127:128], (nb, 128))
        offs = jnp.dot(lower_strict, tot.astype(BF16), preferred_element_type=F32)
        return offs + incl - m

    gt = (bits > thr).astype(F32)
    eq = (bits == thr).astype(F32)
    need = cap - jnp.sum(jnp.sum(gt, axis=1, keepdims=True), axis=0, keepdims=True)
    sel = jnp.maximum(gt, eq * (exclusive_count(eq) < need).astype(F32))
    pos = exclusive_count(sel).astype(jnp.int32)
    posx_ref[...] = pos
    slot_ref[...] = jnp.where(sel > 0.0, pos, -1)


def _slots(aff_t, thr, cap):
    e, n = aff_t.shape
    nb = n // 128
    blk = pl.BlockSpec((nb, 128), lambda i, thr_ref: (i, 0))
    slot, posx = pl.pallas_call(
        functools.partial(_slots_kernel, cap=cap),
        grid_spec=pltpu.PrefetchScalarGridSpec(
            num_scalar_prefetch=1, grid=(e,), in_specs=[blk], out_specs=[blk, blk]),
        out_shape=[jax.ShapeDtypeStruct((e * nb, 128), jnp.int32)] * 2,
        compiler_params=_cparams(1),
        name="moe_slots",
    )(thr[:, 0], aff_t.reshape(e * nb, 128))
    return slot.reshape(e, n), posx.reshape(e, n)


def _gather_kernel(base_ref, slot_ref, hn_ref, xs_ref, *, nsub):
    e, k = pl.program_id(0), pl.program_id(1)

    @pl.when(k == 0)
    def _():
        xs_ref[...] = jnp.zeros_like(xs_ref)

    tile0 = (e * pl.num_programs(1) + k) * nsub
    srow = lax.broadcasted_iota(jnp.int32, (MOE_WIN, MOE_TILE), 0)

    def body(j, carry):
        start = pl.multiple_of((base_ref[tile0 + j] >> 4) << 4, 16)
        onehot = (slot_ref[pl.ds(j, 1), :] - start == srow).astype(BF16)
        tok = hn_ref[pl.ds(pl.multiple_of(j * MOE_TILE, MOE_TILE), MOE_TILE), :]
        rows = jnp.dot(onehot, tok, preferred_element_type=F32).astype(BF16)
        win = pl.ds(start, MOE_WIN)
        xs_ref[win, :] = xs_ref[win, :] + rows
        return carry

    lax.fori_loop(0, nsub, body, 0, unroll=min(4, nsub))


def _gather(hn, slot, base, cap):
    n, dm = hn.shape
    nsub = min(MOE_GATHER_TILES, n // MOE_TILE)
    ngs = n // (nsub * MOE_TILE)
    rows = _slot_rows(cap)
    return pl.pallas_call(
        functools.partial(_gather_kernel, nsub=nsub),
        grid_spec=pltpu.PrefetchScalarGridSpec(
            num_scalar_prefetch=1, grid=(N_EXPERTS, ngs),
            in_specs=[pl.BlockSpec((None, nsub, MOE_TILE), lambda e, k, b: (e * ngs + k, 0, 0)),
                      pl.BlockSpec((nsub * MOE_TILE, dm), lambda e, k, b: (k, 0))],
            out_specs=pl.BlockSpec((None, rows, dm), lambda e, k, b: (e, 0, 0))),
        out_shape=jax.ShapeDtypeStruct((N_EXPERTS, rows, dm), BF16),
        compiler_params=_cparams(2),
        name="moe_gather",
    )(base, slot.reshape(N_EXPERTS * ngs, nsub, MOE_TILE), hn)


def _ffn_kernel(x_ref, w1_ref, w3_ref, w2_ref, o_ref):
    x = x_ref[...]
    a = jnp.dot(x, w1_ref[...], preferred_element_type=F32)
    g = jnp.dot(x, w3_ref[...], preferred_element_type=F32)
    o_ref[...] = jnp.dot((_silu(a) * g).astype(BF16), w2_ref[...], preferred_element_type=F32).astype(BF16)


def _ffn(xs, w1, w3, w2, cap):
    e, rows, dm = xs.shape
    ff = w1.shape[2]
    tr = min(MOE_ROWS, cap)
    return pl.pallas_call(
        _ffn_kernel,
        grid=(e, rows // tr),
        in_specs=[pl.BlockSpec((None, tr, dm), lambda i, r: (i, r, 0)),
                  pl.BlockSpec((None, dm, ff), lambda i, r: (i, 0, 0)),
                  pl.BlockSpec((None, dm, ff), lambda i, r: (i, 0, 0)),
                  pl.BlockSpec((None, ff, dm), lambda i, r: (i, 0, 0))],
        out_specs=pl.BlockSpec((None, tr, dm), lambda i, r: (i, r, 0)),
        out_shape=jax.ShapeDtypeStruct((e, rows, dm), BF16),
        compiler_params=_cparams(2),
        name="moe_ffn",
    )(xs, w1, w3, w2)


def _combine_kernel(base_ref, h_ref, slot_ref, gate_ref, g_ref, *rest, final_norm):
    wins, o_ref = rest[:N_EXPERTS], rest[N_EXPERTS]
    k = pl.program_id(0)
    nt = pl.num_programs(0)
    scol = lax.broadcasted_iota(jnp.int32, (MOE_TILE, MOE_WIN), 1)
    slot, gate = slot_ref[...], gate_ref[...]
    acc = h_ref[...]
    for e in range(N_EXPERTS):
        start = (base_ref[e * nt + k] >> 4) << 4
        pick = jnp.where(slot[:, e:e + 1] - start == scol, gate[:, e:e + 1], 0.0).astype(BF16)
        acc = acc + jnp.dot(pick, wins[e][...], preferred_element_type=F32)
    if final_norm:
        acc = acc * lax.rsqrt(jnp.mean(acc * acc, axis=-1, keepdims=True) + NORM_EPS) * g_ref[...]
    o_ref[...] = acc


def _combine(h, slot_tm, gate_tm, base, outs, final_g, cap):
    n, dm = h.shape
    nt = n // MOE_TILE
    rows = outs.shape[1]
    outs = outs.reshape(N_EXPERTS * rows, dm)
    tile = lambda cols: pl.BlockSpec((MOE_TILE, cols), lambda k, b: (k, 0))

    def window(e):
        return pl.BlockSpec((pl.Element(MOE_WIN), pl.Element(dm)),
                            lambda k, b: (pl.multiple_of(e * rows + ((b[e * nt + k] >> 4) << 4), 16), 0))

    g = jnp.ones((1, dm), F32) if final_g is None else final_g.reshape(1, dm).astype(F32)
    return pl.pallas_call(
        functools.partial(_combine_kernel, final_norm=final_g is not None),
        grid_spec=pltpu.PrefetchScalarGridSpec(
            num_scalar_prefetch=1, grid=(nt,),
            in_specs=[tile(dm), tile(N_EXPERTS), tile(N_EXPERTS), pl.BlockSpec((1, dm), lambda k, b: (0, 0))]
                     + [window(e) for e in range(N_EXPERTS)],
            out_specs=tile(dm)),
        out_shape=jax.ShapeDtypeStruct((n, dm), F32),
        compiler_params=_cparams(1),
        name="moe_combine",
    )(base, h, slot_tm, gate_tm, g, *([outs] * N_EXPERTS))


def _moe(h, norm_g, router_wt, w1, w3, w2, final_g):
    n = h.shape[0]
    cap = CAPACITY_FACTOR * n // N_EXPERTS
    nt = n // MOE_TILE
    hn, aff_t = _router(h, norm_g, router_wt, 512)
    thr = _threshold(aff_t, cap)
    slot, posx = _slots(aff_t, thr, cap)
    base = posx[:, ::MOE_TILE].reshape(N_EXPERTS * nt)
    xs = _gather(hn, slot, base, cap)
    outs = _ffn(xs, w1, w3, w2, cap)
    return _combine(h, slot.T, aff_t.T, base, outs, final_g, cap)


def kernel(x_prompt, x_sample, norm1_g, norm2_g, final_norm_g, w_in, dn_conv_w, dn_a_log, dn_dt_bias, dn_norm_g, dn_proj, s5_lam_re, s5_lam_im, s5_log_step, s5_b_re, s5_b_im, s5_c_re, s5_c_im, s5_d, s5_glu_w, s5_glu_b, s5_proj, rw_mu, rw_w0, rw_w2, rw_a0, rw_a2, rw_g2, rw_k_k, rw_k_a, rw_r_k, rw_ln_g, rw_ln_b, rw_proj, w_out, router_w, expert_w1, expert_w3, expert_w2):
    p = dict(norm1_g=norm1_g, norm2_g=norm2_g, final_norm_g=final_norm_g, w_in=w_in, dn_conv_w=dn_conv_w,
             dn_a_log=dn_a_log, dn_dt_bias=dn_dt_bias, dn_norm_g=dn_norm_g, dn_proj=dn_proj, s5_lam_re=s5_lam_re,
             s5_lam_im=s5_lam_im, s5_log_step=s5_log_step, s5_b_re=s5_b_re, s5_b_im=s5_b_im, s5_c_re=s5_c_re,
             s5_c_im=s5_c_im, s5_d=s5_d, s5_glu_w=s5_glu_w, s5_glu_b=s5_glu_b, s5_proj=s5_proj, rw_mu=rw_mu,
             rw_w0=rw_w0, rw_w2=rw_w2, rw_a0=rw_a0, rw_a2=rw_a2, rw_g2=rw_g2, rw_k_k=rw_k_k, rw_k_a=rw_k_a,
             rw_r_k=rw_r_k, rw_ln_g=rw_ln_g, rw_ln_b=rw_ln_b, rw_proj=rw_proj, w_out=w_out, router_w=router_w,
             expert_w1=expert_w1, expert_w3=expert_w3, expert_w2=expert_w2)
    lp = _layer_params(p, (x_prompt.shape[1], x_sample.shape[1]))
    return _trunk(x_prompt, lp), _trunk(x_sample, lp)


def _layer_params(p, seqlens):
    depth = p['w_in'].shape[0]
    src = _proj_source_columns()
    layers = []
    for layer in range(depth):
        w_in = jnp.where(src[None, :] >= 0, p['w_in'][layer][:, src.clip(0)], 0.0).astype(BF16)
        layers.append({
            'norm1_g': p['norm1_g'][layer], 'norm2_g': p['norm2_g'][layer], 'w_in': w_in,
            'dn': _dn_params(p, layer), 'rw': _rw_params(p, layer), 'merge': _merge_params(p, layer),
            's5': {s: _s5_params(p, layer, s) for s in set(seqlens)},
            'router_wt': p['router_w'][layer].T.astype(BF16),
            'w1': p['expert_w1'][layer].astype(BF16), 'w3': p['expert_w3'][layer].astype(BF16),
            'w2': p['expert_w2'][layer].astype(BF16),
        })
    return {'layers': layers, 'final_norm_g': p['final_norm_g']}


def _trunk(x, lp):
    bsz, seqlen, dm = x.shape
    h = x.reshape(bsz * seqlen, dm)
    tm = min(256, seqlen)
    depth = len(lp['layers'])
    for li, w in enumerate(lp['layers']):
        proj = _norm_proj(h, w['norm1_g'], w['w_in'], min(2048, seqlen), 1024)
        q, k, v, bg = _dn_prep(proj, w['dn'], seqlen, tm)
        o_dn = _dn_scan(q, k, v, bg, bsz, seqlen)
        y_s5 = _s5_core(proj, w['s5'][seqlen], bsz, seqlen)
        r, rv, kk, lw, kd, b, bonus, gate_rw = _rw_prep(proj, w['rw'], seqlen, tm)
        y_rw = _rw_scan(r, rv, kk, lw, kd, b, bsz, seqlen)
        h = _merge(h, proj, o_dn, y_s5, y_rw, bonus, gate_rw, w['merge'], tm)
        h = _moe(h, w['norm2_g'], w['router_wt'], w['w1'], w['w3'], w['w2'],
                 lp['final_norm_g'] if li == depth - 1 else None)
    return h.reshape(bsz, seqlen, dm)
```
